```python
import math
import jax, jax.numpy as jnp
from jax import lax
import numpy as np

D_MODEL = 2048
BATCH = 4
SEQ = 2048
DEPTH = 1
DEC_BATCH = 128
DEC_SEQ = 8
PAST_LEN = 16384
PAGE_SIZE = 128

RET_WIDTH = D_MODEL // 2
SGU_WIDTH = D_MODEL - RET_WIDTH
RET_HEADS = 8
RET_DK = RET_WIDTH // RET_HEADS
RET_DV = RET_WIDTH // RET_HEADS
SGU_GROUPS = 8
SGU_CH = SGU_WIDTH // SGU_GROUPS
CHUNK = 128
D_FF = 4 * D_MODEL
IN_WIDTH = 4 * RET_WIDTH + 2 * SGU_WIDTH
ROPE_THETA = 10000.0
EPS = 1e-6

kernel_name = "hybrid_retention_sgu_decoder_step"


def rmsnorm(x, g):
    xf = x.astype(jnp.float32)
    y = xf * lax.rsqrt(jnp.mean(xf * xf, axis=-1, keepdims=True) + EPS) * g.astype(jnp.float32)
    return y.astype(x.dtype)


def layernorm(x, g, b):
    xf = x.astype(jnp.float32)
    mu = jnp.mean(xf, axis=-1, keepdims=True)
    var = jnp.mean(jnp.square(xf - mu), axis=-1, keepdims=True)
    y = (xf - mu) * lax.rsqrt(var + EPS)
    if g is not None:
        y = y * g.astype(jnp.float32) + b.astype(jnp.float32)
    return y.astype(x.dtype)


def rotary(x, pos):
    d = x.shape[-1]
    inv = 1.0 / (ROPE_THETA ** (jnp.arange(0, d, 2, dtype=jnp.float32) / d))
    ang = pos.astype(jnp.float32)[:, None] * inv[None, :]
    cos = jnp.cos(ang)[None, :, None, :]
    sin = jnp.sin(ang)[None, :, None, :]
    xf = x.astype(jnp.float32)
    x1, x2 = xf[..., : d // 2], xf[..., d // 2:]
    return jnp.concatenate([x1 * cos - x2 * sin, x1 * sin + x2 * cos], axis=-1)


def retention(q, k, v, s0, clen):
    B, L, H, _ = q.shape
    n = L // clen

    def chunks(t):
        return t.astype(jnp.float32).reshape(B, n, clen, H, t.shape[-1]).transpose(1, 0, 3, 2, 4)

    qc, kc, vc = chunks(q), chunks(k), chunks(v)
    lg = jnp.log(1.0 - jnp.power(2.0, -5.0 - jnp.arange(H, dtype=jnp.float32)))
    idx = jnp.arange(clen, dtype=jnp.float32)
    diff = idx[:, None] - idx[None, :]
    dmask = jnp.where(diff[None] >= 0, jnp.exp(jnp.maximum(diff, 0.0)[None] * lg[:, None, None]), 0.0)
    cross = jnp.exp((idx + 1.0)[None, :] * lg[:, None])
    sdec = jnp.exp((clen - 1.0 - idx)[None, :] * lg[:, None])
    cdec = jnp.exp(clen * lg)

    def step(S, inp):
        qb, kb, vb = inp
        scores = jnp.einsum('bhnd,bhmd->bhnm', qb, kb) * dmask[None]
        o = jnp.einsum('bhnm,bhme->bhne', scores, vb) \
            + jnp.einsum('bhnd,bhde->bhne', qb, S) * cross[None, :, :, None]
        S_new = S * cdec[None, :, None, None] \
            + jnp.einsum('bhmd,bhme->bhde', kb * sdec[None, :, :, None], vb)
        return S_new, o

    S, o = lax.scan(step, s0.astype(jnp.float32), (qc, kc, vc))
    o = o.transpose(1, 0, 3, 2, 4).reshape(B, L, H, vc.shape[-1])
    return o, S


def spatial_gate(vn, w_s, b_s, clen):
    B, L, _ = vn.shape
    n = L // clen
    vr = vn.reshape(B, n, clen, SGU_GROUPS, SGU_CH)
    tri = jnp.tril(jnp.ones((clen, clen), dtype=w_s.dtype))
    w = w_s[:, :clen, :clen] * tri[None]
    s = jnp.einsum('gts,bnsgc->bntgc', w, vr) + b_s[:, :clen].T[None, None, :, :, None]
    return s.reshape(B, L, SGU_WIDTH)


def hybrid_layer(x, c, pos, s0, w_ada, b_ada, g_pre_mix, g_post_mix, g_pre_ffn, g_post_ffn,
                 w_in, w_s, b_s, ln_g, ln_b, w_o, w_ff1, w_ff2):
    B, L, _ = x.shape
    dt = x.dtype
    mod = (jax.nn.silu(c.astype(jnp.float32)) @ w_ada.astype(jnp.float32) + b_ada.astype(jnp.float32)).astype(dt)
    sh1, sc1, gt1, sh2, sc2, gt2 = jnp.split(mod[:, None, :], 6, axis=-1)

    h = rmsnorm(x, g_pre_mix) * (1 + sc1) + sh1
    z = h @ w_in
    q, k, v, g, u, vs = jnp.split(z, np.cumsum([RET_WIDTH] * 4 + [SGU_WIDTH])[:].tolist(), axis=-1)
    q = rotary(q.reshape(B, L, RET_HEADS, RET_DK), pos)
    k = rotary(k.reshape(B, L, RET_HEADS, RET_DK), pos) * (RET_DK ** -0.5)
    v = v.reshape(B, L, RET_HEADS, RET_DV)
    clen = min(CHUNK, L)
    o, S = retention(q, k, v, s0, clen)
    o = layernorm(o, None, None).astype(dt).reshape(B, L, RET_WIDTH)
    ret_out = jax.nn.silu(g) * o

    u = jax.nn.gelu(u)
    vn = layernorm(jax.nn.gelu(vs), ln_g, ln_b)
    sgu_out = u * spatial_gate(vn, w_s, b_s, clen)

    m = jnp.concatenate([ret_out, sgu_out], axis=-1) @ w_o
    x = x + gt1 * rmsnorm(m, g_post_mix)

    h2 = rmsnorm(x, g_pre_ffn) * (1 + sc2) + sh2
    f = jnp.square(jax.nn.relu(h2 @ w_ff1)) @ w_ff2
    x = x + gt2 * rmsnorm(f, g_post_ffn)
    return x, S, vn


def setup_inputs(seed: int = 0) -> dict:
    key = jax.random.key(seed)
    ks = jax.random.split(key, 24)
    f32 = jnp.float32
    nrm = lambda k, shape, s: jax.random.normal(k, shape, f32) * s
    return {
        "x_prompt": nrm(ks[0], (BATCH, SEQ, D_MODEL), 1.0),
        "x_sample": nrm(ks[1], (DEC_BATCH, DEC_SEQ, D_MODEL), 1.0),
        "state_ret": nrm(ks[2], (DEPTH, DEC_BATCH, RET_HEADS, RET_DK, RET_DV), 0.1),
        "c_prompt": nrm(ks[3], (BATCH, D_MODEL), 1.0),
        "c_sample": nrm(ks[4], (DEC_BATCH, D_MODEL), 1.0),
        "w_ada": nrm(ks[5], (DEPTH, D_MODEL, 6 * D_MODEL), D_MODEL ** -0.5),
        "b_ada": nrm(ks[6], (DEPTH, 6 * D_MODEL), 0.02),
        "g_pre_mix": 1.0 + nrm(ks[7], (DEPTH, D_MODEL), 0.02),
        "g_post_mix": 1.0 + nrm(ks[8], (DEPTH, D_MODEL), 0.02),
        "g_pre_ffn": 1.0 + nrm(ks[9], (DEPTH, D_MODEL), 0.02),
        "g_post_ffn": 1.0 + nrm(ks[10], (DEPTH, D_MODEL), 0.02),
        "w_in": nrm(ks[11], (DEPTH, D_MODEL, IN_WIDTH), D_MODEL ** -0.5),
        "w_s": nrm(ks[12], (DEPTH, SGU_GROUPS, CHUNK, CHUNK), 0.5 * CHUNK ** -0.5),
        "b_s": 1.0 + nrm(ks[13], (DEPTH, SGU_GROUPS, CHUNK), 0.02),
        "ln_g": 1.0 + nrm(ks[14], (DEPTH, SGU_WIDTH), 0.02),
        "ln_b": nrm(ks[15], (DEPTH, SGU_WIDTH), 0.02),
        "w_o": nrm(ks[16], (DEPTH, RET_WIDTH + SGU_WIDTH, D_MODEL), (RET_WIDTH + SGU_WIDTH) ** -0.5),
        "w_ff1": nrm(ks[17], (DEPTH, D_MODEL, D_FF), D_MODEL ** -0.5),
        "w_ff2": nrm(ks[18], (DEPTH, D_FF, D_MODEL), D_FF ** -0.5),
    }


def reference(x_prompt, x_sample, state_ret, c_prompt, c_sample, w_ada, b_ada, g_pre_mix,
              g_post_mix, g_pre_ffn, g_post_ffn, w_in, w_s, b_s, ln_g, ln_b, w_o, w_ff1, w_ff2):
    pos_prompt = jnp.arange(SEQ, dtype=jnp.int32)
    pos_sample = PAST_LEN + jnp.arange(DEC_SEQ, dtype=jnp.int32)
    yp, ys = x_prompt, x_sample
    sp_list, ss_list, vs_list = [], [], []
    for l in range(DEPTH):
        w = (w_ada[l], b_ada[l], g_pre_mix[l], g_post_mix[l], g_pre_ffn[l], g_post_ffn[l],
             w_in[l], w_s[l], b_s[l], ln_g[l], ln_b[l], w_o[l], w_ff1[l], w_ff2[l])
        s0_prompt = jnp.zeros((BATCH, RET_HEADS, RET_DK, RET_DV), jnp.float32)
        yp, sp, _ = hybrid_layer(yp, c_prompt, pos_prompt, s0_prompt, *w)
        ys, ss, vn_s = hybrid_layer(ys, c_sample, pos_sample, state_ret[l], *w)
        sp_list.append(sp)
        ss_list.append(ss)
        vs_list.append(vn_s)
    state_ret_prompt = jnp.stack(sp_list)
    state_ret_sample = jnp.stack(ss_list)
    sgu_v_sample = jnp.stack(vs_list)
    return (yp, ys, state_ret_prompt, state_ret_sample, sgu_v_sample)
```

```python
import functools

import numpy as np
import jax
import jax.numpy as jnp
from jax import lax
from jax.experimental import pallas as pl
from jax.experimental.pallas import tpu as pltpu

D_MODEL = 2048
RET_WIDTH = 1024
SGU_WIDTH = 1024
HEADS = 8
HEAD_DIM = 128
GROUPS = 8
CHUNK = 128
D_FF = 4 * D_MODEL
IN_WIDTH = 4 * RET_WIDTH + 2 * SGU_WIDTH
ROPE_THETA = 10000.0
EPS = 1e-6
PAST_LEN = 16384

F32 = jnp.float32
BF16 = jnp.bfloat16

VMEM_LIMIT_BIG = 52 * 1024 * 1024
VMEM_LIMIT_SMALL = 40 * 1024 * 1024


def _decay_tables(clen):
    lg = np.log(1.0 - np.power(2.0, -5.0 - np.arange(HEADS, dtype=np.float64)))
    idx = np.arange(clen, dtype=np.float64)
    diff = idx[:, None] - idx[None, :]
    dmask = np.where(diff[None] >= 0, np.exp(np.maximum(diff, 0.0)[None] * lg[:, None, None]), 0.0)
    cross = np.exp((idx + 1.0)[None, :] * lg[:, None])
    sdec = np.exp((clen - 1.0 - idx)[None, :] * lg[:, None])
    cdec = np.exp(clen * lg)
    crossb = np.broadcast_to(cross[:, :, None], (HEADS, clen, HEAD_DIM))
    sdecb = np.broadcast_to(sdec[:, :, None], (HEADS, clen, HEAD_DIM))
    return (dmask.astype(np.float32), np.ascontiguousarray(crossb, dtype=np.float32),
            np.ascontiguousarray(sdecb, dtype=np.float32), [float(c) for c in cdec])


def _rotary_tables(pos):
    inv = 1.0 / (ROPE_THETA ** (np.arange(0, HEAD_DIM, 2, dtype=np.float64) / HEAD_DIM))
    ang = np.asarray(pos, dtype=np.float64)[:, None] * inv[None, :]
    cos, sin = np.cos(ang), np.sin(ang)
    cos2 = np.concatenate([cos, cos], axis=-1).astype(np.float32)
    sin2 = np.concatenate([-sin, sin], axis=-1).astype(np.float32)
    return cos2, sin2


def _rms_scale(x):
    return lax.rsqrt(jnp.mean(x * x, axis=-1, keepdims=True) + EPS)


def _layernorm(x):
    mu = jnp.mean(x, axis=-1, keepdims=True)
    xc = x - mu
    var = jnp.mean(xc * xc, axis=-1, keepdims=True)
    return xc * lax.rsqrt(var + EPS)


def _rotary(x, cos2, sin2):
    return x * cos2 + pltpu.roll(x, HEAD_DIM // 2, 1) * sin2


def _ada_kernel(c_ref, w_ref, b_ref, o_ref):
    c = c_ref[...]
    s = (c * jax.nn.sigmoid(c)).astype(BF16)
    o_ref[...] = jnp.dot(s, w_ref[...].astype(BF16), preferred_element_type=F32) + b_ref[...]


def _ada(c_all, w_ada, b_ada):
    n_rows = c_all.shape[0]
    tn = 1024
    n_out = w_ada.shape[1]
    return pl.pallas_call(
        _ada_kernel,
        grid=(n_out // tn,),
        in_specs=[
            pl.BlockSpec((n_rows, D_MODEL), lambda j: (0, 0)),
            pl.BlockSpec((D_MODEL, tn), lambda j: (0, j)),
            pl.BlockSpec((1, tn), lambda j: (0, j)),
        ],
        out_specs=pl.BlockSpec((n_rows, tn), lambda j: (0, j)),
        out_shape=jax.ShapeDtypeStruct((n_rows, n_out), F32),
        compiler_params=pltpu.CompilerParams(
            dimension_semantics=("arbitrary",), vmem_limit_bytes=VMEM_LIMIT_SMALL),
        name="ada_modulation",
    )(c_all, w_ada, b_ada.reshape(1, n_out))


NORM_ROWS = 128


def _norm_matmul_kernel(x_ref, g_ref, sc_ref, sh_ref, w_ref, z_ref, h_ref, *, mod_rows):
    tm = x_ref.shape[0]

    @pl.when(pl.program_id(1) == 0)
    def _():
        g = g_ref[...]

        def body(r, carry):
            rows = pl.ds(pl.multiple_of(r * NORM_ROWS, NORM_ROWS), NORM_ROWS)
            mrows = rows if mod_rows else slice(None)
            x = x_ref[rows, :]
            h = x * _rms_scale(x) * g * (1.0 + sc_ref[mrows, :]) + sh_ref[mrows, :]
            h_ref[rows, :] = h.astype(BF16)
            return carry

        lax.fori_loop(0, tm // NORM_ROWS, body, 0)

    z_ref[...] = jnp.dot(h_ref[...], w_ref[...], preferred_element_type=F32)


def _mod_spec(mod, tm, index_of_tile):
    rows = mod.shape[1]
    return pl.BlockSpec((None, rows, D_MODEL), index_of_tile)


def _norm_matmul(x, g, sc, sh, w, *, tm, tiles_per_mod):
    t = x.shape[0]
    tn = 1024
    n = w.shape[1]
    mod_idx = lambda i, j: (i // tiles_per_mod, 0, 0)
    kern = functools.partial(_norm_matmul_kernel, mod_rows=sc.shape[1] != 1)
    return pl.pallas_call(
        kern,
        grid=(t // tm, n // tn),
        in_specs=[
            pl.BlockSpec((tm, D_MODEL), lambda i, j: (i, 0)),
            pl.BlockSpec((1, D_MODEL), lambda i, j: (0, 0)),
            _mod_spec(sc, tm, mod_idx),
            _mod_spec(sh, tm, mod_idx),
            pl.BlockSpec((D_MODEL, tn), lambda i, j: (0, j)),
        ],
        out_specs=pl.BlockSpec((tm, tn), lambda i, j: (i, j)),
        out_shape=jax.ShapeDtypeStruct((t, n), F32),
        scratch_shapes=[pltpu.VMEM((tm, D_MODEL), BF16)],
        compiler_params=pltpu.CompilerParams(
            dimension_semantics=("arbitrary", "arbitrary"), vmem_limit_bytes=VMEM_LIMIT_BIG),
        name="norm_in_proj",
    )(x, g, sc, sh, w)


def _mix_prompt_kernel(z_ref, cos_ref, sin_ref, dmask_ref, cross_ref, sdec_ref, ws_ref, bias_ref,
                       lng_ref, lnb_ref, o_ref, sout_ref, s_ref, *, cdec):
    c = pl.program_id(1)

    @pl.when(c == 0)
    def _():
        s_ref[...] = jnp.zeros_like(s_ref)

    cos2 = cos_ref[...]
    sin2 = sin_ref[...]
    for h in range(HEADS):
        lo = h * HEAD_DIM
        q = _rotary(z_ref[:, lo:lo + HEAD_DIM], cos2, sin2)
        k = _rotary(z_ref[:, RET_WIDTH + lo:RET_WIDTH + lo + HEAD_DIM], cos2, sin2) * (HEAD_DIM ** -0.5)
        v = z_ref[:, 2 * RET_WIDTH + lo:2 * RET_WIDTH + lo + HEAD_DIM].astype(BF16)
        g = z_ref[:, 3 * RET_WIDTH + lo:3 * RET_WIDTH + lo + HEAD_DIM]
        qb = q.astype(BF16)
        kb = k.astype(BF16)
        scores = lax.dot_general(qb, kb, (((1,), (1,)), ((), ())),
                                 preferred_element_type=F32) * dmask_ref[h]
        state = s_ref[h]
        o = (jnp.dot(scores.astype(BF16), v, preferred_element_type=F32)
             + jnp.dot(qb, state.astype(BF16), preferred_element_type=F32) * cross_ref[h])
        kd = (k * sdec_ref[h]).astype(BF16)
        s_ref[h] = state * cdec[h] + lax.dot_general(kd, v, (((0,), (0,)), ((), ())),
                                                     preferred_element_type=F32)
        o_ref[:, lo:lo + HEAD_DIM] = (g * jax.nn.sigmoid(g) * _layernorm(o)).astype(o_ref.dtype)

    u = jax.nn.gelu(z_ref[:, 4 * RET_WIDTH:4 * RET_WIDTH + SGU_WIDTH])
    vn = _layernorm(jax.nn.gelu(z_ref[:, 4 * RET_WIDTH + SGU_WIDTH:])) * lng_ref[...] + lnb_ref[...]
    row = lax.broadcasted_iota(jnp.int32, (CHUNK, CHUNK), 0)
    col = lax.broadcasted_iota(jnp.int32, (CHUNK, CHUNK), 1)
    causal = row >= col
    for gi in range(GROUPS):
        lo = gi * CHUNK
        w = jnp.where(causal, ws_ref[gi], 0.0).astype(BF16)
        s = jnp.dot(w, vn[:, lo:lo + CHUNK].astype(BF16), preferred_element_type=F32) + bias_ref[:, lo:lo + CHUNK]
        o_ref[:, RET_WIDTH + lo:RET_WIDTH + lo + CHUNK] = (u[:, lo:lo + CHUNK] * s).astype(o_ref.dtype)

    @pl.when(c == pl.num_programs(1) - 1)
    def _():
        sout_ref[...] = s_ref[...]


def _mix_prompt(z, w_s, bias_tc, ln_g, ln_b, batch, seq):
    n_chunks = seq // CHUNK
    cos2, sin2 = _rotary_tables(np.arange(seq))
    dmask, crossb, sdecb, cdec = _decay_tables(CHUNK)
    const3 = lambda b, c: (0, 0, 0)
    const2 = lambda b, c: (0, 0)
    kern = functools.partial(_mix_prompt_kernel, cdec=cdec)
    return pl.pallas_call(
        kern,
        grid=(batch, n_chunks),
        in_specs=[
            pl.BlockSpec((None, CHUNK, IN_WIDTH), lambda b, c: (b, c, 0)),
            pl.BlockSpec((CHUNK, HEAD_DIM), lambda b, c: (c, 0)),
            pl.BlockSpec((CHUNK, HEAD_DIM), lambda b, c: (c, 0)),
            pl.BlockSpec((HEADS, CHUNK, CHUNK), const3),
            pl.BlockSpec((HEADS, CHUNK, HEAD_DIM), const3),
            pl.BlockSpec((HEADS, CHUNK, HEAD_DIM), const3),
            pl.BlockSpec((GROUPS, CHUNK, CHUNK), const3),
            pl.BlockSpec((CHUNK, SGU_WIDTH), const2),
            pl.BlockSpec((1, SGU_WIDTH), const2),
            pl.BlockSpec((1, SGU_WIDTH), const2),
        ],
        out_specs=[
            pl.BlockSpec((None, CHUNK, D_MODEL), lambda b, c: (b, c, 0)),
            pl.BlockSpec((None, HEADS, HEAD_DIM, HEAD_DIM), lambda b, c: (b, 0, 0, 0)),
        ],
        out_shape=[
            jax.ShapeDtypeStruct((batch, seq, D_MODEL), BF16),
            jax.ShapeDtypeStruct((batch, HEADS, HEAD_DIM, HEAD_DIM), F32),
        ],
        scratch_shapes=[pltpu.VMEM((HEADS, HEAD_DIM, HEAD_DIM), F32)],
        compiler_params=pltpu.CompilerParams(
            dimension_semantics=("arbitrary", "arbitrary"), vmem_limit_bytes=VMEM_LIMIT_SMALL),
        name="mix_prompt",
    )(z.reshape(batch, seq, IN_WIDTH), jnp.asarray(cos2), jnp.asarray(sin2), jnp.asarray(dmask),
      jnp.asarray(crossb), jnp.asarray(sdecb), w_s, bias_tc, ln_g, ln_b)


def _mix_sample_kernel(z_ref, state_ref, cos_ref, sin_ref, dmask_ref, cross_ref, sdec_ref, wt_ref,
                       bias_ref, lng_ref, lnb_ref, o_ref, sout_ref, vn_ref, *, cdec, seq):
    bb = z_ref.shape[0]

    u = jax.nn.gelu(z_ref[:, :, 4 * RET_WIDTH:4 * RET_WIDTH + SGU_WIDTH])
    vn = _layernorm(jax.nn.gelu(z_ref[:, :, 4 * RET_WIDTH + SGU_WIDTH:])) * lng_ref[...] + lnb_ref[...]
    vn_ref[...] = vn
    trow = lax.broadcasted_iota(jnp.int32, (seq, SGU_WIDTH), 0)
    s = jnp.broadcast_to(bias_ref[...], (bb, seq, SGU_WIDTH))
    for src in range(seq):
        w = jnp.where(trow >= src, wt_ref[src], 0.0)
        s = s + w * vn[:, src:src + 1, :]
    o_ref[:, :, RET_WIDTH:] = u * s

    cos2 = cos_ref[...]
    sin2 = sin_ref[...]

    def body(b, carry):
        for h in range(HEADS):
            lo = h * HEAD_DIM
            q = _rotary(z_ref[b, :, lo:lo + HEAD_DIM], cos2, sin2)
            k = _rotary(z_ref[b, :, RET_WIDTH + lo:RET_WIDTH + lo + HEAD_DIM], cos2, sin2) * (HEAD_DIM ** -0.5)
            v = z_ref[b, :, 2 * RET_WIDTH + lo:2 * RET_WIDTH + lo + HEAD_DIM]
            g = z_ref[b, :, 3 * RET_WIDTH + lo:3 * RET_WIDTH + lo + HEAD_DIM]
            scores = lax.dot_general(q, k, (((1,), (1,)), ((), ())),
                                     preferred_element_type=F32) * dmask_ref[h]
            state = state_ref[b, h]
            o = (jnp.dot(scores, v, preferred_element_type=F32)
                 + jnp.dot(q.astype(BF16), state.astype(BF16), preferred_element_type=F32) * cross_ref[h])
            kd = k * sdec_ref[h]
            sout_ref[b, h] = state * cdec[h] + lax.dot_general(kd, v, (((0,), (0,)), ((), ())),
                                                               preferred_element_type=F32)
            o_ref[b, :, lo:lo + HEAD_DIM] = g * jax.nn.sigmoid(g) * _layernorm(o)
        return carry

    lax.fori_loop(0, bb, body, 0)


def _mix_sample(z, state, wt, bias_t, ln_g, ln_b, batch, seq):
    bb = 8
    cos2, sin2 = _rotary_tables(PAST_LEN + np.arange(seq))
    dmask, crossb, sdecb, cdec = _decay_tables(seq)
    const3 = lambda i: (0, 0, 0)
    const2 = lambda i: (0, 0)
    kern = functools.partial(_mix_sample_kernel, cdec=cdec, seq=seq)
    return pl.pallas_call(
        kern,
        grid=(batch // bb,),
        in_specs=[
            pl.BlockSpec((bb, seq, IN_WIDTH), lambda i: (i, 0, 0)),
            pl.BlockSpec((bb, HEADS, HEAD_DIM, HEAD_DIM), lambda i: (i, 0, 0, 0)),
            pl.BlockSpec((seq, HEAD_DIM), const2),
            pl.BlockSpec((seq, HEAD_DIM), const2),
            pl.BlockSpec((HEADS, seq, seq), const3),
            pl.BlockSpec((HEADS, seq, HEAD_DIM), const3),
            pl.BlockSpec((HEADS, seq, HEAD_DIM), const3),
            pl.BlockSpec((seq, seq, SGU_WIDTH), const3),
            pl.BlockSpec((seq, SGU_WIDTH), const2),
            pl.BlockSpec((1, SGU_WIDTH), const2),
            pl.BlockSpec((1, SGU_WIDTH), const2),
        ],
        out_specs=[
            pl.BlockSpec((bb, seq, D_MODEL), lambda i: (i, 0, 0)),
            pl.BlockSpec((bb, HEADS, HEAD_DIM, HEAD_DIM), lambda i: (i, 0, 0, 0)),
            pl.BlockSpec((bb, seq, SGU_WIDTH), lambda i: (i, 0, 0)),
        ],
        out_shape=[
            jax.ShapeDtypeStruct((batch, seq, D_MODEL), F32),
            jax.ShapeDtypeStruct((batch, HEADS, HEAD_DIM, HEAD_DIM), F32),
            jax.ShapeDtypeStruct((batch, seq, SGU_WIDTH), F32),
        ],
        compiler_params=pltpu.CompilerParams(
            dimension_semantics=("arbitrary",), vmem_limit_bytes=VMEM_LIMIT_SMALL),
        name="mix_sample",
    )(z.reshape(batch, seq, IN_WIDTH), state, jnp.asarray(cos2), jnp.asarray(sin2), jnp.asarray(dmask),
      jnp.asarray(crossb), jnp.asarray(sdecb), wt, bias_t, ln_g, ln_b)


OUT_ROWS = 256


def _out_proj_kernel(m_ref, w_ref, x_ref, gpost_ref, gt_ref, gpre_ref, sc_ref, sh_ref,
                     x1_ref, h2_ref, *, mod_rows):
    tm = x_ref.shape[0]
    for r in range(tm // OUT_ROWS):
        rows = slice(r * OUT_ROWS, (r + 1) * OUT_ROWS)
        mrows = rows if mod_rows else slice(None)
        m = jnp.dot(m_ref[rows, :].astype(BF16), w_ref[...], preferred_element_type=F32)
        x1 = x_ref[rows, :] + gt_ref[mrows, :] * (m * _rms_scale(m) * gpost_ref[...])
        x1_ref[rows, :] = x1
        h2 = x1 * _rms_scale(x1) * gpre_ref[...] * (1.0 + sc_ref[mrows, :]) + sh_ref[mrows, :]
        h2_ref[rows, :] = h2.astype(BF16)


def _out_proj(m, w, x, gpost, gt, gpre, sc, sh, *, tm, tiles_per_mod):
    t = x.shape[0]
    mod_idx = lambda i: (i // tiles_per_mod, 0, 0)
    vec = pl.BlockSpec((1, D_MODEL), lambda i: (0, 0))
    kern = functools.partial(_out_proj_kernel, mod_rows=gt.shape[1] != 1)
    return pl.pallas_call(
        kern,
        grid=(t // tm,),
        in_specs=[
            pl.BlockSpec((tm, D_MODEL), lambda i: (i, 0)),
            pl.BlockSpec((D_MODEL, D_MODEL), lambda i: (0, 0)),
            pl.BlockSpec((tm, D_MODEL), lambda i: (i, 0)),
            vec,
            _mod_spec(gt, tm, mod_idx),
            vec,
            _mod_spec(sc, tm, mod_idx),
            _mod_spec(sh, tm, mod_idx),
        ],
        out_specs=[
            pl.BlockSpec((tm, D_MODEL), lambda i: (i, 0)),
            pl.BlockSpec((tm, D_MODEL), lambda i: (i, 0)),
        ],
        out_shape=[
            jax.ShapeDtypeStruct((t, D_MODEL), F32),
            jax.ShapeDtypeStruct((t, D_MODEL), BF16),
        ],
        compiler_params=pltpu.CompilerParams(
            dimension_semantics=("arbitrary",), vmem_limit_bytes=VMEM_LIMIT_BIG),
        name="out_proj",
    )(m, w, x, gpost, gt, gpre, sc, sh)


def _ffn_kernel(h2_ref, w1_ref, w2_ref, x1_ref, g_ref, gt_ref, y_ref):
    k = pl.program_id(1)
    a = jnp.dot(h2_ref[...], w1_ref[...], preferred_element_type=F32)
    a = jnp.square(jnp.maximum(a, 0.0)).astype(BF16)
    p = jnp.dot(a, w2_ref[...], preferred_element_type=F32)

    @pl.when(k == 0)
    def _():
        y_ref[...] = p

    @pl.when(k > 0)
    def _():
        y_ref[...] += p

    @pl.when(k == pl.num_programs(1) - 1)
    def _():
        f = y_ref[...]
        y_ref[...] = x1_ref[...] + gt_ref[...] * (f * _rms_scale(f) * g_ref[...])


def _ffn(h2, w1, w2, x1, g, gt, *, tm, tiles_per_mod):
    t = h2.shape[0]
    tc = 1024
    mod_idx = lambda i, k: (i // tiles_per_mod, 0, 0)
    return pl.pallas_call(
        _ffn_kernel,
        grid=(t // tm, D_FF // tc),
        in_specs=[
            pl.BlockSpec((tm, D_MODEL), lambda i, k: (i, 0)),
            pl.BlockSpec((D_MODEL, tc), lambda i, k: (0, k)),
            pl.BlockSpec((tc, D_MODEL), lambda i, k: (k, 0)),
            pl.BlockSpec((tm, D_MODEL), lambda i, k: (i, 0)),
            pl.BlockSpec((1, D_MODEL), lambda i, k: (0, 0)),
            _mod_spec(gt, tm, mod_idx),
        ],
        out_specs=pl.BlockSpec((tm, D_MODEL), lambda i, k: (i, 0)),
        out_shape=jax.ShapeDtypeStruct((t, D_MODEL), F32),
        compiler_params=pltpu.CompilerParams(
            dimension_semantics=("arbitrary", "arbitrary"), vmem_limit_bytes=VMEM_LIMIT_BIG),
        name="ffn",
    )(h2, w1, w2, x1, g, gt)


def _layer(x, mods, state, weights, *, batch, seq, per_token_mod):
    (g_pre_mix, g_post_mix, g_pre_ffn, g_post_ffn, w_in, w_s, bias_tc, wt, bias_t, ln_g, ln_b,
     w_o, w_ff1, w_ff2) = weights
    sh1, sc1, gt1, sh2, sc2, gt2 = mods
    t = x.shape[0]
    tm_in = min(1024, t)
    tm = 512
    if per_token_mod:
        per = lambda tile: 1
    else:
        per = lambda tile: seq // tile

    def shaped(mod, tile):
        return mod.reshape(t // tile, tile, D_MODEL) if per_token_mod else mod

    z = _norm_matmul(x, g_pre_mix, shaped(sc1, tm_in), shaped(sh1, tm_in), w_in,
                     tm=tm_in, tiles_per_mod=per(tm_in))
    if state is None:
        mix, s_new = _mix_prompt(z, w_s, bias_tc, ln_g, ln_b, batch, seq)
        vn = None
    else:
        mix, s_new, vn = _mix_sample(z, state, wt, bias_t, ln_g, ln_b, batch, seq)
    tm_out = OUT_ROWS if per_token_mod else tm
    x1, h2 = _out_proj(mix.reshape(t, D_MODEL), w_o, x, g_post_mix, shaped(gt1, tm_out), g_pre_ffn,
                       shaped(sc2, tm_out), shaped(sh2, tm_out), tm=tm_out, tiles_per_mod=per(tm_out))
    y = _ffn(h2, w_ff1, w_ff2, x1, g_post_ffn, shaped(gt2, tm), tm=tm, tiles_per_mod=per(tm))
    return y, s_new, vn


def kernel(x_prompt, x_sample, state_ret, c_prompt, c_sample, w_ada, b_ada, g_pre_mix, g_post_mix,
           g_pre_ffn, g_post_ffn, w_in, w_s, b_s, ln_g, ln_b, w_o, w_ff1, w_ff2):
    depth = w_ada.shape[0]
    batch, seq, _ = x_prompt.shape
    dec_batch, dec_seq, _ = x_sample.shape
    yp = x_prompt.reshape(batch * seq, D_MODEL)
    ys = x_sample.reshape(dec_batch * dec_seq, D_MODEL)
    c_all = jnp.concatenate([c_prompt, c_sample], axis=0)
    sp_list, ss_list, vs_list = [], [], []
    for l in range(depth):
        mod = _ada(c_all, w_ada[l], b_ada[l])
        mods_p = [m.reshape(batch, 1, D_MODEL) for m in jnp.split(mod[:batch], 6, axis=-1)]
        mods_s = [jnp.repeat(m, dec_seq, axis=0) for m in jnp.split(mod[batch:], 6, axis=-1)]
        bias_tc = jnp.repeat(b_s[l].T, CHUNK, axis=1)
        wt = jnp.repeat(jnp.transpose(w_s[l][:, :dec_seq, :dec_seq], (2, 1, 0)), CHUNK, axis=2)
        bias_t = bias_tc[:dec_seq]
        row = lambda v: v.reshape(1, -1)
        weights = (row(g_pre_mix[l]), row(g_post_mix[l]), row(g_pre_ffn[l]), row(g_post_ffn[l]),
                   w_in[l].astype(BF16), w_s[l], bias_tc, wt, bias_t, row(ln_g[l]), row(ln_b[l]),
                   w_o[l].astype(BF16), w_ff1[l].astype(BF16), w_ff2[l].astype(BF16))
        yp, sp, _ = _layer(yp, mods_p, None, weights, batch=batch, seq=seq, per_token_mod=False)
        ys, ss, vn = _layer(ys, mods_s, state_ret[l], weights, batch=dec_batch, seq=dec_seq,
                            per_token_mod=True)
        sp_list.append(sp)
        ss_list.append(ss)
        vs_list.append(vn)
    return (yp.reshape(batch, seq, D_MODEL), ys.reshape(dec_batch, dec_seq, D_MODEL),
            jnp.stack(sp_list), jnp.stack(ss_list), jnp.stack(vs_list))
```

```python
import functools

import numpy as np
import jax
import jax.numpy as jnp
from jax import lax
from jax.experimental import pallas as pl
from jax.experimental.pallas import tpu as pltpu

D_MODEL = 2048
RET_WIDTH = 1024
SGU_WIDTH = 1024
HEADS = 8
HEAD_DIM = 128
GROUPS = 8
CHUNK = 128
D_FF = 4 * D_MODEL
IN_WIDTH = 4 * RET_WIDTH + 2 * SGU_WIDTH
ROPE_THETA = 10000.0
EPS = 1e-6
PAST_LEN = 16384

F32 = jnp.float32
BF16 = jnp.bfloat16

VMEM_LIMIT_BIG = 52 * 1024 * 1024
VMEM_LIMIT_SMALL = 40 * 1024 * 1024


def _decay_tables(clen):
    lg = np.log(1.0 - np.power(2.0, -5.0 - np.arange(HEADS, dtype=np.float64)))
    idx = np.arange(clen, dtype=np.float64)
    diff = idx[:, None] - idx[None, :]
    dmask = np.where(diff[None] >= 0, np.exp(np.maximum(diff, 0.0)[None] * lg[:, None, None]), 0.0)
    cross = np.exp((idx + 1.0)[None, :] * lg[:, None])
    sdec = np.exp((clen - 1.0 - idx)[None, :] * lg[:, None])
    cdec = np.exp(clen * lg)
    crossb = np.broadcast_to(cross[:, :, None], (HEADS, clen, HEAD_DIM))
    sdecb = np.broadcast_to(sdec[:, :, None], (HEADS, clen, HEAD_DIM))
    return (dmask.astype(np.float32), np.ascontiguousarray(crossb, dtype=np.float32),
            np.ascontiguousarray(sdecb, dtype=np.float32), [float(c) for c in cdec])


def _rotary_tables(pos):
    inv = 1.0 / (ROPE_THETA ** (np.arange(0, HEAD_DIM, 2, dtype=np.float64) / HEAD_DIM))
    ang = np.asarray(pos, dtype=np.float64)[:, None] * inv[None, :]
    cos, sin = np.cos(ang), np.sin(ang)
    cos2 = np.concatenate([cos, cos], axis=-1).astype(np.float32)
    sin2 = np.concatenate([-sin, sin], axis=-1).astype(np.float32)
    return cos2, sin2


def _rms_scale(x):
    return lax.rsqrt(jnp.mean(x * x, axis=-1, keepdims=True) + EPS)


def _layernorm(x):
    mu = jnp.mean(x, axis=-1, keepdims=True)
    xc = x - mu
    var = jnp.mean(xc * xc, axis=-1, keepdims=True)
    return xc * lax.rsqrt(var + EPS)


def _rotary(x, cos2, sin2):
    return x * cos2 + pltpu.roll(x, HEAD_DIM // 2, 1) * sin2


def _ada_kernel(c_ref, w_ref, b_ref, o_ref):
    c = c_ref[...]
    s = (c * jax.nn.sigmoid(c)).astype(BF16)
    o_ref[...] = jnp.dot(s, w_ref[...].astype(BF16), preferred_element_type=F32) + b_ref[...]


def _ada(c_all, w_ada, b_ada):
    n_rows = c_all.shape[0]
    tn = 1024
    n_out = w_ada.shape[1]
    return pl.pallas_call(
        _ada_kernel,
        grid=(n_out // tn,),
        in_specs=[
            pl.BlockSpec((n_rows, D_MODEL), lambda j: (0, 0)),
            pl.BlockSpec((D_MODEL, tn), lambda j: (0, j)),
            pl.BlockSpec((1, tn), lambda j: (0, j)),
        ],
        out_specs=pl.BlockSpec((n_rows, tn), lambda j: (0, j)),
        out_shape=jax.ShapeDtypeStruct((n_rows, n_out), F32),
        compiler_params=pltpu.CompilerParams(
            dimension_semantics=("arbitrary",), vmem_limit_bytes=VMEM_LIMIT_SMALL),
        name="ada_modulation",
    )(c_all, w_ada, b_ada.reshape(1, n_out))


NORM_ROWS = 256


def _norm_matmul_kernel(x_ref, g_ref, sc_ref, sh_ref, w_ref, z_ref, h_ref, *, mod_rows):
    tm = x_ref.shape[0]

    @pl.when(pl.program_id(1) == 0)
    def _():
        g = g_ref[...]
        for r in range(tm // NORM_ROWS):
            rows = slice(r * NORM_ROWS, (r + 1) * NORM_ROWS)
            mrows = rows if mod_rows else slice(None)
            x = x_ref[rows, :]
            h = (x * _rms_scale(x) * g * (1.0 + sc_ref[mrows, :]) + sh_ref[mrows, :]).astype(BF16)
            h_ref[rows, :] = h
            z_ref[rows, :] = jnp.dot(h, w_ref[...], preferred_element_type=F32)

    @pl.when(pl.program_id(1) != 0)
    def _():
        z_ref[...] = jnp.dot(h_ref[...], w_ref[...], preferred_element_type=F32)


def _mod_spec(mod, tm, index_of_tile):
    rows = mod.shape[1]
    return pl.BlockSpec((None, rows, D_MODEL), index_of_tile)


def _norm_matmul(x, g, sc, sh, w, *, tm, tiles_per_mod):
    t = x.shape[0]
    tn = 1024
    n = w.shape[1]
    mod_idx = lambda i, j: (i // tiles_per_mod, 0, 0)
    kern = functools.partial(_norm_matmul_kernel, mod_rows=sc.shape[1] != 1)
    return pl.pallas_call(
        kern,
        grid=(t // tm, n // tn),
        in_specs=[
            pl.BlockSpec((tm, D_MODEL), lambda i, j: (i, 0)),
            pl.BlockSpec((1, D_MODEL), lambda i, j: (0, 0)),
            _mod_spec(sc, tm, mod_idx),
            _mod_spec(sh, tm, mod_idx),
            pl.BlockSpec((D_MODEL, tn), lambda i, j: (0, j)),
        ],
        out_specs=pl.BlockSpec((tm, tn), lambda i, j: (i, j)),
        out_shape=jax.ShapeDtypeStruct((t, n), F32),
        scratch_shapes=[pltpu.VMEM((tm, D_MODEL), BF16)],
        compiler_params=pltpu.CompilerParams(
            dimension_semantics=("arbitrary", "arbitrary"), vmem_limit_bytes=VMEM_LIMIT_BIG),
        name="norm_in_proj",
    )(x, g, sc, sh, w)


def _mix_prompt_kernel(z_ref, cos_ref, sin_ref, dmask_ref, cross_ref, sdec_ref, ws_ref, bias_ref,
                       lng_ref, lnb_ref, o_ref, sout_ref, s_ref, *, cdec):
    c = pl.program_id(1)

    @pl.when(c == 0)
    def _():
        s_ref[...] = jnp.zeros_like(s_ref)

    cos2 = cos_ref[...]
    sin2 = sin_ref[...]
    for h in range(HEADS):
        lo = h * HEAD_DIM
        q = _rotary(z_ref[:, lo:lo + HEAD_DIM], cos2, sin2)
        k = _rotary(z_ref[:, RET_WIDTH + lo:RET_WIDTH + lo + HEAD_DIM], cos2, sin2) * (HEAD_DIM ** -0.5)
        v = z_ref[:, 2 * RET_WIDTH + lo:2 * RET_WIDTH + lo + HEAD_DIM].astype(BF16)
        g = z_ref[:, 3 * RET_WIDTH + lo:3 * RET_WIDTH + lo + HEAD_DIM]
        qb = q.astype(BF16)
        kb = k.astype(BF16)
        scores = lax.dot_general(qb, kb, (((1,), (1,)), ((), ())),
                                 preferred_element_type=F32) * dmask_ref[h]
        state = s_ref[h]
        o = (jnp.dot(scores.astype(BF16), v, preferred_element_type=F32)
             + jnp.dot(qb, state.astype(BF16), preferred_element_type=F32) * cross_ref[h])
        kd = (k * sdec_ref[h]).astype(BF16)
        s_ref[h] = state * cdec[h] + lax.dot_general(kd, v, (((0,), (0,)), ((), ())),
                                                     preferred_element_type=F32)
        o_ref[:, lo:lo + HEAD_DIM] = (g * jax.nn.sigmoid(g) * _layernorm(o)).astype(o_ref.dtype)

    u = jax.nn.gelu(z_ref[:, 4 * RET_WIDTH:4 * RET_WIDTH + SGU_WIDTH])
    vn = _layernorm(jax.nn.gelu(z_ref[:, 4 * RET_WIDTH + SGU_WIDTH:])) * lng_ref[...] + lnb_ref[...]
    row = lax.broadcasted_iota(jnp.int32, (CHUNK, CHUNK), 0)
    col = lax.broadcasted_iota(jnp.int32, (CHUNK, CHUNK), 1)
    causal = row >= col
    for gi in range(GROUPS):
        lo = gi * CHUNK
        w = jnp.where(causal, ws_ref[gi], 0.0).astype(BF16)
        s = jnp.dot(w, vn[:, lo:lo + CHUNK].astype(BF16), preferred_element_type=F32) + bias_ref[:, lo:lo + CHUNK]
        o_ref[:, RET_WIDTH + lo:RET_WIDTH + lo + CHUNK] = (u[:, lo:lo + CHUNK] * s).astype(o_ref.dtype)

    @pl.when(c == pl.num_programs(1) - 1)
    def _():
        sout_ref[...] = s_ref[...]


def _mix_prompt(z, w_s, bias_tc, ln_g, ln_b, batch, seq):
    n_chunks = seq // CHUNK
    cos2, sin2 = _rotary_tables(np.arange(seq))
    dmask, crossb, sdecb, cdec = _decay_tables(CHUNK)
    const3 = lambda b, c: (0, 0, 0)
    const2 = lambda b, c: (0, 0)
    kern = functools.partial(_mix_prompt_kernel, cdec=cdec)
    return pl.pallas_call(
        kern,
        grid=(batch, n_chunks),
        in_specs=[
            pl.BlockSpec((None, CHUNK, IN_WIDTH), lambda b, c: (b, c, 0)),
            pl.BlockSpec((CHUNK, HEAD_DIM), lambda b, c: (c, 0)),
            pl.BlockSpec((CHUNK, HEAD_DIM), lambda b, c: (c, 0)),
            pl.BlockSpec((HEADS, CHUNK, CHUNK), const3),
            pl.BlockSpec((HEADS, CHUNK, HEAD_DIM), const3),
            pl.BlockSpec((HEADS, CHUNK, HEAD_DIM), const3),
            pl.BlockSpec((GROUPS, CHUNK, CHUNK), const3),
            pl.BlockSpec((CHUNK, SGU_WIDTH), const2),
            pl.BlockSpec((1, SGU_WIDTH), const2),
            pl.BlockSpec((1, SGU_WIDTH), const2),
        ],
        out_specs=[
            pl.BlockSpec((None, CHUNK, D_MODEL), lambda b, c: (b, c, 0)),
            pl.BlockSpec((None, HEADS, HEAD_DIM, HEAD_DIM), lambda b, c: (b, 0, 0, 0)),
        ],
        out_shape=[
            jax.ShapeDtypeStruct((batch, seq, D_MODEL), BF16),
            jax.ShapeDtypeStruct((batch, HEADS, HEAD_DIM, HEAD_DIM), F32),
        ],
        scratch_shapes=[pltpu.VMEM((HEADS, HEAD_DIM, HEAD_DIM), F32)],
        compiler_params=pltpu.CompilerParams(
            dimension_semantics=("arbitrary", "arbitrary"), vmem_limit_bytes=VMEM_LIMIT_SMALL),
        name="mix_prompt",
    )(z.reshape(batch, seq, IN_WIDTH), jnp.asarray(cos2), jnp.asarray(sin2), jnp.asarray(dmask),
      jnp.asarray(crossb), jnp.asarray(sdecb), w_s, bias_tc, ln_g, ln_b)


def _block_diag(x, mask):
    return jnp.where(mask, jnp.concatenate([x] * HEADS, axis=0), 0.0)


def _heads_to_rows(x):
    return jnp.concatenate([x[:, h * HEAD_DIM:(h + 1) * HEAD_DIM] for h in range(HEADS)], axis=0)


def _mix_sample_kernel(z_ref, state_ref, cos_ref, sin_ref, dmask_ref, cross_ref, sdec_ref, wt_ref,
                       bias_ref, lng_ref, lnb_ref, o_ref, sout_ref, vn_ref, *, cdec, seq):
    bb = z_ref.shape[0]

    u = jax.nn.gelu(z_ref[:, :, 4 * RET_WIDTH:4 * RET_WIDTH + SGU_WIDTH])
    vn = _layernorm(jax.nn.gelu(z_ref[:, :, 4 * RET_WIDTH + SGU_WIDTH:])) * lng_ref[...] + lnb_ref[...]
    vn_ref[...] = vn
    trow = lax.broadcasted_iota(jnp.int32, (seq, SGU_WIDTH), 0)
    s = jnp.broadcast_to(bias_ref[...], (bb, seq, SGU_WIDTH))
    for src in range(seq):
        w = jnp.where(trow >= src, wt_ref[src], 0.0)
        s = s + w * vn[:, src:src + 1, :]
    o_ref[:, :, RET_WIDTH:] = u * s

    cos2 = cos_ref[...]
    sin2 = sin_ref[...]
    rblk = lax.broadcasted_iota(jnp.int32, (HEADS * seq, RET_WIDTH), 0) // seq
    cblk = lax.broadcasted_iota(jnp.int32, (HEADS * seq, RET_WIDTH), 1) // HEAD_DIM
    diag = rblk == cblk

    def rotary_heads(base, b):
        return jnp.concatenate(
            [_rotary(z_ref[b, :, base + h * HEAD_DIM:base + (h + 1) * HEAD_DIM], cos2, sin2)
             for h in range(HEADS)], axis=1)

    def body(b, carry):
        q = rotary_heads(0, b)
        k = rotary_heads(RET_WIDTH, b) * (HEAD_DIM ** -0.5)
        v_rows = _heads_to_rows(z_ref[b, :, 2 * RET_WIDTH:3 * RET_WIDTH]).astype(BF16)
        g_rows = _heads_to_rows(z_ref[b, :, 3 * RET_WIDTH:4 * RET_WIDTH])
        qbd = _block_diag(q, diag).astype(BF16)
        kbd = _block_diag(k, diag).astype(BF16)
        kdbd = _block_diag(k * sdec_ref[...], diag).astype(BF16)
        scores = lax.dot_general(qbd, kbd, (((1,), (1,)), ((), ())),
                                 preferred_element_type=F32) * dmask_ref[...]
        state = state_ref[b].reshape(HEADS * HEAD_DIM, HEAD_DIM)
        o = (jnp.dot(scores.astype(BF16), v_rows, preferred_element_type=F32)
             + jnp.dot(qbd, state.astype(BF16), preferred_element_type=F32) * cross_ref[...])
        out = g_rows * jax.nn.sigmoid(g_rows) * _layernorm(o)
        ds = lax.dot_general(kdbd, v_rows, (((0,), (0,)), ((), ())), preferred_element_type=F32)
        for h in range(HEADS):
            o_ref[b, :, h * HEAD_DIM:(h + 1) * HEAD_DIM] = out[h * seq:(h + 1) * seq]
            sout_ref[b, h] = (state_ref[b, h] * cdec[h]
                              + ds[h * HEAD_DIM:(h + 1) * HEAD_DIM])
        return carry

    lax.fori_loop(0, bb, body, 0, unroll=True)


def _mix_sample(z, state, wt, bias_t, ln_g, ln_b, batch, seq):
    bb = 8
    cos2, sin2 = _rotary_tables(PAST_LEN + np.arange(seq))
    dmask, crossb, sdecb, cdec = _decay_tables(seq)
    dm = np.zeros((HEADS * seq, HEADS * seq), np.float32)
    for h in range(HEADS):
        dm[h * seq:(h + 1) * seq, h * seq:(h + 1) * seq] = dmask[h]
    cross_rows = crossb.reshape(HEADS * seq, HEAD_DIM)
    sdec_cols = np.ascontiguousarray(np.transpose(sdecb, (1, 0, 2)).reshape(seq, RET_WIDTH))
    const3 = lambda i: (0, 0, 0)
    const2 = lambda i: (0, 0)
    kern = functools.partial(_mix_sample_kernel, cdec=cdec, seq=seq)
    return pl.pallas_call(
        kern,
        grid=(batch // bb,),
        in_specs=[
            pl.BlockSpec((bb, seq, IN_WIDTH), lambda i: (i, 0, 0)),
            pl.BlockSpec((bb, HEADS, HEAD_DIM, HEAD_DIM), lambda i: (i, 0, 0, 0)),
            pl.BlockSpec((seq, HEAD_DIM), const2),
            pl.BlockSpec((seq, HEAD_DIM), const2),
            pl.BlockSpec((HEADS * seq, HEADS * seq), const2),
            pl.BlockSpec((HEADS * seq, HEAD_DIM), const2),
            pl.BlockSpec((seq, RET_WIDTH), const2),
            pl.BlockSpec((seq, seq, SGU_WIDTH), const3),
            pl.BlockSpec((seq, SGU_WIDTH), const2),
            pl.BlockSpec((1, SGU_WIDTH), const2),
            pl.BlockSpec((1, SGU_WIDTH), const2),
        ],
        out_specs=[
            pl.BlockSpec((bb, seq, D_MODEL), lambda i: (i, 0, 0)),
            pl.BlockSpec((bb, HEADS, HEAD_DIM, HEAD_DIM), lambda i: (i, 0, 0, 0)),
            pl.BlockSpec((bb, seq, SGU_WIDTH), lambda i: (i, 0, 0)),
        ],
        out_shape=[
            jax.ShapeDtypeStruct((batch, seq, D_MODEL), F32),
            jax.ShapeDtypeStruct((batch, HEADS, HEAD_DIM, HEAD_DIM), F32),
            jax.ShapeDtypeStruct((batch, seq, SGU_WIDTH), F32),
        ],
        compiler_params=pltpu.CompilerParams(
            dimension_semantics=("arbitrary",), vmem_limit_bytes=VMEM_LIMIT_SMALL),
        name="mix_sample",
    )(z.reshape(batch, seq, IN_WIDTH), state, jnp.asarray(cos2), jnp.asarray(sin2), jnp.asarray(dm),
      jnp.asarray(cross_rows), jnp.asarray(sdec_cols), wt, bias_t, ln_g, ln_b)


OUT_ROWS = 256


def _out_proj_kernel(m_ref, w_ref, x_ref, gpost_ref, gt_ref, gpre_ref, sc_ref, sh_ref,
                     x1_ref, h2_ref, *, mod_rows):
    tm = x_ref.shape[0]
    for r in range(tm // OUT_ROWS):
        rows = slice(r * OUT_ROWS, (r + 1) * OUT_ROWS)
        mrows = rows if mod_rows else slice(None)
        m = jnp.dot(m_ref[rows, :].astype(BF16), w_ref[...], preferred_element_type=F32)
        x1 = x_ref[rows, :] + gt_ref[mrows, :] * (m * _rms_scale(m) * gpost_ref[...])
        x1_ref[rows, :] = x1
        h2 = x1 * _rms_scale(x1) * gpre_ref[...] * (1.0 + sc_ref[mrows, :]) + sh_ref[mrows, :]
        h2_ref[rows, :] = h2.astype(BF16)


def _out_proj(m, w, x, gpost, gt, gpre, sc, sh, *, tm, tiles_per_mod):
    t = x.shape[0]
    mod_idx = lambda i: (i // tiles_per_mod, 0, 0)
    vec = pl.BlockSpec((1, D_MODEL), lambda i: (0, 0))
    kern = functools.partial(_out_proj_kernel, mod_rows=gt.shape[1] != 1)
    return pl.pallas_call(
        kern,
        grid=(t // tm,),
        in_specs=[
            pl.BlockSpec((tm, D_MODEL), lambda i: (i, 0)),
            pl.BlockSpec((D_MODEL, D_MODEL), lambda i: (0, 0)),
            pl.BlockSpec((tm, D_MODEL), lambda i: (i, 0)),
            vec,
            _mod_spec(gt, tm, mod_idx),
            vec,
            _mod_spec(sc, tm, mod_idx),
            _mod_spec(sh, tm, mod_idx),
        ],
        out_specs=[
            pl.BlockSpec((tm, D_MODEL), lambda i: (i, 0)),
            pl.BlockSpec((tm, D_MODEL), lambda i: (i, 0)),
        ],
        out_shape=[
            jax.ShapeDtypeStruct((t, D_MODEL), F32),
            jax.ShapeDtypeStruct((t, D_MODEL), BF16),
        ],
        compiler_params=pltpu.CompilerParams(
            dimension_semantics=("arbitrary",), vmem_limit_bytes=VMEM_LIMIT_BIG),
        name="out_proj",
    )(m, w, x, gpost, gt, gpre, sc, sh)


def _ffn_kernel(h2_ref, w1_ref, w2_ref, x1_ref, g_ref, gt_ref, y_ref):
    k = pl.program_id(1)

    @pl.when(k == 0)
    def _():
        y_ref[...] = jnp.zeros_like(y_ref)

    a = jnp.dot(h2_ref[...], w1_ref[...], preferred_element_type=F32)
    a = jnp.square(jnp.maximum(a, 0.0)).astype(BF16)
    y_ref[...] += jnp.dot(a, w2_ref[...], preferred_element_type=F32)

    @pl.when(k == pl.num_programs(1) - 1)
    def _():
        f = y_ref[...]
        y_ref[...] = x1_ref[...] + gt_ref[...] * (f * _rms_scale(f) * g_ref[...])


def _ffn(h2, w1, w2, x1, g, gt, *, tm, tiles_per_mod):
    t = h2.shape[0]
    tc = 1024
    mod_idx = lambda i, k: (i // tiles_per_mod, 0, 0)
    return pl.pallas_call(
        _ffn_kernel,
        grid=(t // tm, D_FF // tc),
        in_specs=[
            pl.BlockSpec((tm, D_MODEL), lambda i, k: (i, 0)),
            pl.BlockSpec((D_MODEL, tc), lambda i, k: (0, k)),
            pl.BlockSpec((tc, D_MODEL), lambda i, k: (k, 0)),
            pl.BlockSpec((tm, D_MODEL), lambda i, k: (i, 0)),
            pl.BlockSpec((1, D_MODEL), lambda i, k: (0, 0)),
            _mod_spec(gt, tm, mod_idx),
        ],
        out_specs=pl.BlockSpec((tm, D_MODEL), lambda i, k: (i, 0)),
        out_shape=jax.ShapeDtypeStruct((t, D_MODEL), F32),
        compiler_params=pltpu.CompilerParams(
            dimension_semantics=("arbitrary", "arbitrary"), vmem_limit_bytes=VMEM_LIMIT_BIG),
        name="ffn",
    )(h2, w1, w2, x1, g, gt)


def _layer(x, mods, state, weights, *, batch, seq, per_token_mod):
    (g_pre_mix, g_post_mix, g_pre_ffn, g_post_ffn, w_in, w_s, bias_tc, wt, bias_t, ln_g, ln_b,
     w_o, w_ff1, w_ff2) = weights
    sh1, sc1, gt1, sh2, sc2, gt2 = mods
    t = x.shape[0]
    tm_in = min(1024, t)
    tm = 512
    if per_token_mod:
        per = lambda tile: 1
    else:
        per = lambda tile: seq // tile

    def shaped(mod, tile):
        return mod.reshape(t // tile, tile, D_MODEL) if per_token_mod else mod

    z = _norm_matmul(x, g_pre_mix, shaped(sc1, tm_in), shaped(sh1, tm_in), w_in,
                     tm=tm_in, tiles_per_mod=per(tm_in))
    if state is None:
        mix, s_new = _mix_prompt(z, w_s, bias_tc, ln_g, ln_b, batch, seq)
        vn = None
    else:
        mix, s_new, vn = _mix_sample(z, state, wt, bias_t, ln_g, ln_b, batch, seq)
    tm_out = OUT_ROWS if per_token_mod else tm
    x1, h2 = _out_proj(mix.reshape(t, D_MODEL), w_o, x, g_post_mix, shaped(gt1, tm_out), g_pre_ffn,
                       shaped(sc2, tm_out), shaped(sh2, tm_out), tm=tm_out, tiles_per_mod=per(tm_out))
    y = _ffn(h2, w_ff1, w_ff2, x1, g_post_ffn, shaped(gt2, tm), tm=tm, tiles_per_mod=per(tm))
    return y, s_new, vn


def kernel(x_prompt, x_sample, state_ret, c_prompt, c_sample, w_ada, b_ada, g_pre_mix, g_post_mix,
           g_pre_ffn, g_post_ffn, w_in, w_s, b_s, ln_g, ln_b, w_o, w_ff1, w_ff2):
    depth = w_ada.shape[0]
    batch, seq, _ = x_prompt.shape
    dec_batch, dec_seq, _ = x_sample.shape
    yp = x_prompt.reshape(batch * seq, D_MODEL)
    ys = x_sample.reshape(dec_batch * dec_seq, D_MODEL)
    c_all = jnp.concatenate([c_prompt, c_sample], axis=0)
    sp_list, ss_list, vs_list = [], [], []
    for l in range(depth):
        mod = _ada(c_all, w_ada[l], b_ada[l])
        mods_p = [m.reshape(batch, 1, D_MODEL) for m in jnp.split(mod[:batch], 6, axis=-1)]
        mods_s = [jnp.repeat(m, dec_seq, axis=0) for m in jnp.split(mod[batch:], 6, axis=-1)]
        bias_tc = jnp.repeat(b_s[l].T, CHUNK, axis=1)
        wt = jnp.repeat(jnp.transpose(w_s[l][:, :dec_seq, :dec_seq], (2, 1, 0)), CHUNK, axis=2)
        bias_t = bias_tc[:dec_seq]
        row = lambda v: v.reshape(1, -1)
        weights = (row(g_pre_mix[l]), row(g_post_mix[l]), row(g_pre_ffn[l]), row(g_post_ffn[l]),
                   w_in[l].astype(BF16), w_s[l], bias_tc, wt, bias_t, row(ln_g[l]), row(ln_b[l]),
                   w_o[l].astype(BF16), w_ff1[l].astype(BF16), w_ff2[l].astype(BF16))
        yp, sp, _ = _layer(yp, mods_p, None, weights, batch=batch, seq=seq, per_token_mod=False)
        ys, ss, vn = _layer(ys, mods_s, state_ret[l], weights, batch=dec_batch, seq=dec_seq,
                            per_token_mod=True)
        sp_list.append(sp)
        ss_list.append(ss)
        vs_list.append(vn)
    return (yp.reshape(batch, seq, D_MODEL), ys.reshape(dec_batch, dec_seq, D_MODEL),
            jnp.stack(sp_list), jnp.stack(ss_list), jnp.stack(vs_list))
```

```python
import functools

import numpy as np
import jax
import jax.numpy as jnp
from jax import lax
from jax.experimental import pallas as pl
from jax.experimental.pallas import tpu as pltpu

D_MODEL = 2048
RET_WIDTH = 1024
SGU_WIDTH = 1024
HEADS = 8
HEAD_DIM = 128
GROUPS = 8
CHUNK = 128
D_FF = 4 * D_MODEL
IN_WIDTH = 4 * RET_WIDTH + 2 * SGU_WIDTH
ROPE_THETA = 10000.0
EPS = 1e-6
PAST_LEN = 16384

F32 = jnp.float32
BF16 = jnp.bfloat16

VMEM_LIMIT_BIG = 58 * 1024 * 1024
VMEM_LIMIT_SMALL = 40 * 1024 * 1024


def _decay_tables(clen):
    lg = np.log(1.0 - np.power(2.0, -5.0 - np.arange(HEADS, dtype=np.float64)))
    idx = np.arange(clen, dtype=np.float64)
    diff = idx[:, None] - idx[None, :]
    dmask = np.where(diff[None] >= 0, np.exp(np.maximum(diff, 0.0)[None] * lg[:, None, None]), 0.0)
    cross = np.exp((idx + 1.0)[None, :] * lg[:, None])
    sdec = np.exp((clen - 1.0 - idx)[None, :] * lg[:, None])
    cdec = np.exp(clen * lg)
    crossb = np.broadcast_to(cross[:, :, None], (HEADS, clen, HEAD_DIM))
    sdecb = np.broadcast_to(sdec[:, :, None], (HEADS, clen, HEAD_DIM))
    return (dmask.astype(np.float32), np.ascontiguousarray(crossb, dtype=np.float32),
            np.ascontiguousarray(sdecb, dtype=np.float32), [float(c) for c in cdec])


def _rotary_tables(pos):
    inv = 1.0 / (ROPE_THETA ** (np.arange(0, HEAD_DIM, 2, dtype=np.float64) / HEAD_DIM))
    ang = np.asarray(pos, dtype=np.float64)[:, None] * inv[None, :]
    cos, sin = np.cos(ang), np.sin(ang)
    cos2 = np.concatenate([cos, cos], axis=-1).astype(np.float32)
    sin2 = np.concatenate([-sin, sin], axis=-1).astype(np.float32)
    return cos2, sin2


def _rms_scale(x):
    return lax.rsqrt(jnp.mean(x * x, axis=-1, keepdims=True) + EPS)


def _layernorm(x):
    mu = jnp.mean(x, axis=-1, keepdims=True)
    xc = x - mu
    var = jnp.mean(xc * xc, axis=-1, keepdims=True)
    return xc * lax.rsqrt(var + EPS)


def _rotary(x, cos2, sin2):
    return x * cos2 + pltpu.roll(x, HEAD_DIM // 2, 1) * sin2


def _ada_kernel(c_ref, w_ref, b_ref, o_ref):
    c = c_ref[...]
    s = (c * jax.nn.sigmoid(c)).astype(BF16)
    o_ref[...] = jnp.dot(s, w_ref[...].astype(BF16), preferred_element_type=F32) + b_ref[...]


def _ada(c_all, w_ada, b_ada):
    n_rows = c_all.shape[0]
    tn = 1024
    n_out = w_ada.shape[1]
    return pl.pallas_call(
        _ada_kernel,
        grid=(n_out // tn,),
        in_specs=[
            pl.BlockSpec((n_rows, D_MODEL), lambda j: (0, 0)),
            pl.BlockSpec((D_MODEL, tn), lambda j: (0, j)),
            pl.BlockSpec((1, tn), lambda j: (0, j)),
        ],
        out_specs=pl.BlockSpec((n_rows, tn), lambda j: (0, j)),
        out_shape=jax.ShapeDtypeStruct((n_rows, n_out), F32),
        compiler_params=pltpu.CompilerParams(
            dimension_semantics=("arbitrary",), vmem_limit_bytes=VMEM_LIMIT_SMALL),
        name="ada_modulation",
    )(c_all, w_ada, b_ada.reshape(1, n_out))


NORM_ROWS = 256


def _norm_matmul_steps(x_ref, g_ref, sc_ref, sh_ref, w_ref, z_ref, h_ref, mod_rows):
    tm = x_ref.shape[0]

    @pl.when(pl.program_id(1) == 0)
    def _():
        g = g_ref[...]
        for r in range(tm // NORM_ROWS):
            rows = slice(r * NORM_ROWS, (r + 1) * NORM_ROWS)
            mrows = rows if mod_rows else slice(None)
            x = x_ref[rows, :]
            h = (x * _rms_scale(x) * g * (1.0 + sc_ref[mrows, :]) + sh_ref[mrows, :]).astype(BF16)
            h_ref[rows, :] = h
            z_ref[rows, :] = jnp.dot(h, w_ref[...], preferred_element_type=F32)

    @pl.when(pl.program_id(1) != 0)
    def _():
        z_ref[...] = jnp.dot(h_ref[...], w_ref[...], preferred_element_type=F32)


def _norm_matmul_cast_kernel(x_ref, g_ref, sc_ref, sh_ref, w32_ref, z_ref, wbf_ref, h_ref, *, mod_rows):
    wbf_ref[...] = w32_ref[...].astype(BF16)
    _norm_matmul_steps(x_ref, g_ref, sc_ref, sh_ref, wbf_ref, z_ref, h_ref, mod_rows)


def _norm_matmul_side_kernel(x_ref, g_ref, sc_ref, sh_ref, w_ref, a32_ref, b32_ref, c32_ref,
                             z_ref, abf_ref, bbf_ref, cbf_ref, h_ref, *, mod_rows, side_steps):
    @pl.when(pl.program_id(1) < side_steps)
    def _():
        abf_ref[...] = a32_ref[...].astype(BF16)
        bbf_ref[...] = b32_ref[...].astype(BF16)
        cbf_ref[...] = c32_ref[...].astype(BF16)

    _norm_matmul_steps(x_ref, g_ref, sc_ref, sh_ref, w_ref, z_ref, h_ref, mod_rows)


def _mod_spec(mod, tm, index_of_tile):
    rows = mod.shape[1]
    return pl.BlockSpec((None, rows, D_MODEL), index_of_tile)


def _norm_matmul_specs(sc, sh, tm, tn, tiles_per_mod):
    mod_idx = lambda i, j: (i // tiles_per_mod, 0, 0)
    return [
        pl.BlockSpec((tm, D_MODEL), lambda i, j: (i, 0)),
        pl.BlockSpec((1, D_MODEL), lambda i, j: (0, 0)),
        _mod_spec(sc, tm, mod_idx),
        _mod_spec(sh, tm, mod_idx),
        pl.BlockSpec((D_MODEL, tn), lambda i, j: (0, j)),
    ]


def _norm_matmul_cast(x, g, sc, sh, w32, *, tiles_per_mod):
    t = x.shape[0]
    tn = 512
    n = w32.shape[1]
    kern = functools.partial(_norm_matmul_cast_kernel, mod_rows=sc.shape[1] != 1)
    return pl.pallas_call(
        kern,
        grid=(1, n // tn),
        in_specs=_norm_matmul_specs(sc, sh, t, tn, tiles_per_mod),
        out_specs=[
            pl.BlockSpec((t, tn), lambda i, j: (i, j)),
            pl.BlockSpec((D_MODEL, tn), lambda i, j: (0, j)),
        ],
        out_shape=[
            jax.ShapeDtypeStruct((t, n), F32),
            jax.ShapeDtypeStruct(w32.shape, BF16),
        ],
        scratch_shapes=[pltpu.VMEM((t, D_MODEL), BF16)],
        compiler_params=pltpu.CompilerParams(
            dimension_semantics=("arbitrary", "arbitrary"), vmem_limit_bytes=VMEM_LIMIT_BIG),
        name="norm_in_proj_cast",
    )(x, g, sc, sh, w32)


SIDE_STEPS = 4


def _norm_matmul_side(x, g, sc, sh, w, side32, *, tm, tiles_per_mod):
    t = x.shape[0]
    tn = 1024
    n = w.shape[1]
    n_slabs = (t // tm) * SIDE_STEPS
    slab_idx = lambda i, j: (i * SIDE_STEPS + jnp.minimum(j, SIDE_STEPS - 1), 0)
    side_specs = [pl.BlockSpec((a.shape[0] // n_slabs, a.shape[1]), slab_idx) for a in side32]
    kern = functools.partial(_norm_matmul_side_kernel, mod_rows=sc.shape[1] != 1, side_steps=SIDE_STEPS)
    return pl.pallas_call(
        kern,
        grid=(t // tm, n // tn),
        in_specs=_norm_matmul_specs(sc, sh, tm, tn, tiles_per_mod) + side_specs,
        out_specs=[pl.BlockSpec((tm, tn), lambda i, j: (i, j))] + side_specs,
        out_shape=[jax.ShapeDtypeStruct((t, n), F32)]
                  + [jax.ShapeDtypeStruct(a.shape, BF16) for a in side32],
        scratch_shapes=[pltpu.VMEM((tm, D_MODEL), BF16)],
        compiler_params=pltpu.CompilerParams(
            dimension_semantics=("arbitrary", "arbitrary"), vmem_limit_bytes=VMEM_LIMIT_BIG),
        name="norm_in_proj",
    )(x, g, sc, sh, w, *side32)


def _mix_prompt_kernel(z_ref, cos_ref, sin_ref, dmask_ref, cross_ref, sdec_ref, ws_ref, bias_ref,
                       lng_ref, lnb_ref, o_ref, sout_ref, s_ref, *, cdec):
    c = pl.program_id(1)

    @pl.when(c == 0)
    def _():
        s_ref[...] = jnp.zeros_like(s_ref)

    row = lax.broadcasted_iota(jnp.int32, (CHUNK, CHUNK), 0)
    col = lax.broadcasted_iota(jnp.int32, (CHUNK, CHUNK), 1)
    causal = row >= col
    for ci in range(z_ref.shape[0] // CHUNK):
        rows = slice(ci * CHUNK, (ci + 1) * CHUNK)
        cos2 = cos_ref[rows, :]
        sin2 = sin_ref[rows, :]
        for h in range(HEADS):
            lo = h * HEAD_DIM
            q = _rotary(z_ref[rows, lo:lo + HEAD_DIM], cos2, sin2)
            k = _rotary(z_ref[rows, RET_WIDTH + lo:RET_WIDTH + lo + HEAD_DIM], cos2, sin2) * (HEAD_DIM ** -0.5)
            v = z_ref[rows, 2 * RET_WIDTH + lo:2 * RET_WIDTH + lo + HEAD_DIM].astype(BF16)
            g = z_ref[rows, 3 * RET_WIDTH + lo:3 * RET_WIDTH + lo + HEAD_DIM]
            qb = q.astype(BF16)
            kb = k.astype(BF16)
            scores = lax.dot_general(qb, kb, (((1,), (1,)), ((), ())),
                                     preferred_element_type=F32) * dmask_ref[h]
            state = s_ref[h]
            o = (jnp.dot(scores.astype(BF16), v, preferred_element_type=F32)
                 + jnp.dot(qb, state.astype(BF16), preferred_element_type=F32) * cross_ref[h])
            kd = (k * sdec_ref[h]).astype(BF16)
            s_ref[h] = state * cdec[h] + lax.dot_general(kd, v, (((0,), (0,)), ((), ())),
                                                         preferred_element_type=F32)
            o_ref[rows, lo:lo + HEAD_DIM] = (g * jax.nn.sigmoid(g) * _layernorm(o)).astype(o_ref.dtype)

        u = jax.nn.gelu(z_ref[rows, 4 * RET_WIDTH:4 * RET_WIDTH + SGU_WIDTH])
        vn = (_layernorm(jax.nn.gelu(z_ref[rows, 4 * RET_WIDTH + SGU_WIDTH:])) * lng_ref[...]
              + lnb_ref[...])
        for gi in range(GROUPS):
            lo = gi * CHUNK
            w = jnp.where(causal, ws_ref[gi], 0.0).astype(BF16)
            s = (jnp.dot(w, vn[:, lo:lo + CHUNK].astype(BF16), preferred_element_type=F32)
                 + bias_ref[:, lo:lo + CHUNK])
            o_ref[rows, RET_WIDTH + lo:RET_WIDTH + lo + CHUNK] = (u[:, lo:lo + CHUNK] * s).astype(o_ref.dtype)

    @pl.when(c == pl.num_programs(1) - 1)
    def _():
        sout_ref[...] = s_ref[...]


MIX_CHUNKS_PER_STEP = 4


def _mix_prompt(z, w_s, bias_tc, ln_g, ln_b, batch, seq):
    rows = MIX_CHUNKS_PER_STEP * CHUNK
    cos2, sin2 = _rotary_tables(np.arange(seq))
    dmask, crossb, sdecb, cdec = _decay_tables(CHUNK)
    const3 = lambda b, c: (0, 0, 0)
    const2 = lambda b, c: (0, 0)
    kern = functools.partial(_mix_prompt_kernel, cdec=cdec)
    return pl.pallas_call(
        kern,
        grid=(batch, seq // rows),
        in_specs=[
            pl.BlockSpec((None, rows, IN_WIDTH), lambda b, c: (b, c, 0)),
            pl.BlockSpec((rows, HEAD_DIM), lambda b, c: (c, 0)),
            pl.BlockSpec((rows, HEAD_DIM), lambda b, c: (c, 0)),
            pl.BlockSpec((HEADS, CHUNK, CHUNK), const3),
            pl.BlockSpec((HEADS, CHUNK, HEAD_DIM), const3),
            pl.BlockSpec((HEADS, CHUNK, HEAD_DIM), const3),
            pl.BlockSpec((GROUPS, CHUNK, CHUNK), const3),
            pl.BlockSpec((CHUNK, SGU_WIDTH), const2),
            pl.BlockSpec((1, SGU_WIDTH), const2),
            pl.BlockSpec((1, SGU_WIDTH), const2),
        ],
        out_specs=[
            pl.BlockSpec((None, rows, D_MODEL), lambda b, c: (b, c, 0)),
            pl.BlockSpec((None, HEADS, HEAD_DIM, HEAD_DIM), lambda b, c: (b, 0, 0, 0)),
        ],
        out_shape=[
            jax.ShapeDtypeStruct((batch, seq, D_MODEL), BF16),
            jax.ShapeDtypeStruct((batch, HEADS, HEAD_DIM, HEAD_DIM), F32),
        ],
        scratch_shapes=[pltpu.VMEM((HEADS, HEAD_DIM, HEAD_DIM), F32)],
        compiler_params=pltpu.CompilerParams(
            dimension_semantics=("arbitrary", "arbitrary"), vmem_limit_bytes=VMEM_LIMIT_SMALL),
        name="mix_prompt",
    )(z.reshape(batch, seq, IN_WIDTH), jnp.asarray(cos2), jnp.asarray(sin2), jnp.asarray(dmask),
      jnp.asarray(crossb), jnp.asarray(sdecb), w_s, bias_tc, ln_g, ln_b)


def _block_diag(x, mask):
    return jnp.where(mask, jnp.concatenate([x] * HEADS, axis=0), 0.0)


def _heads_to_rows(x):
    return jnp.concatenate([x[:, h * HEAD_DIM:(h + 1) * HEAD_DIM] for h in range(HEADS)], axis=0)


def _mix_sample_kernel(z_ref, state_ref, cos_ref, sin_ref, dmask_ref, cross_ref, sdec_ref, wt_ref,
                       bias_ref, lng_ref, lnb_ref, o_ref, sout_ref, vn_ref, *, cdec, seq):
    bb = z_ref.shape[0]

    u = jax.nn.gelu(z_ref[:, :, 4 * RET_WIDTH:4 * RET_WIDTH + SGU_WIDTH])
    vn = _layernorm(jax.nn.gelu(z_ref[:, :, 4 * RET_WIDTH + SGU_WIDTH:])) * lng_ref[...] + lnb_ref[...]
    vn_ref[...] = vn
    trow = lax.broadcasted_iota(jnp.int32, (seq, SGU_WIDTH), 0)
    s = jnp.broadcast_to(bias_ref[...], (bb, seq, SGU_WIDTH))
    for src in range(seq):
        w = jnp.where(trow >= src, wt_ref[src], 0.0)
        s = s + w * vn[:, src:src + 1, :]
    o_ref[:, :, RET_WIDTH:] = u * s

    cos2 = cos_ref[...]
    sin2 = sin_ref[...]
    rblk = lax.broadcasted_iota(jnp.int32, (HEADS * seq, RET_WIDTH), 0) // seq
    cblk = lax.broadcasted_iota(jnp.int32, (HEADS * seq, RET_WIDTH), 1) // HEAD_DIM
    diag = rblk == cblk

    def rotary_heads(base, b):
        return jnp.concatenate(
            [_rotary(z_ref[b, :, base + h * HEAD_DIM:base + (h + 1) * HEAD_DIM], cos2, sin2)
             for h in range(HEADS)], axis=1)

    def body(b, carry):
        q = rotary_heads(0, b)
        k = rotary_heads(RET_WIDTH, b) * (HEAD_DIM ** -0.5)
        v_rows = _heads_to_rows(z_ref[b, :, 2 * RET_WIDTH:3 * RET_WIDTH]).astype(BF16)
        g_rows = _heads_to_rows(z_ref[b, :, 3 * RET_WIDTH:4 * RET_WIDTH])
        qbd = _block_diag(q, diag).astype(BF16)
        kbd = _block_diag(k, diag).astype(BF16)
        kdbd = _block_diag(k * sdec_ref[...], diag).astype(BF16)
        scores = lax.dot_general(qbd, kbd, (((1,), (1,)), ((), ())),
                                 preferred_element_type=F32) * dmask_ref[...]
        state = state_ref[b].reshape(HEADS * HEAD_DIM, HEAD_DIM)
        o = (jnp.dot(scores.astype(BF16), v_rows, preferred_element_type=F32)
             + jnp.dot(qbd, state.astype(BF16), preferred_element_type=F32) * cross_ref[...])
        out = g_rows * jax.nn.sigmoid(g_rows) * _layernorm(o)
        ds = lax.dot_general(kdbd, v_rows, (((0,), (0,)), ((), ())), preferred_element_type=F32)
        for h in range(HEADS):
            o_ref[b, :, h * HEAD_DIM:(h + 1) * HEAD_DIM] = out[h * seq:(h + 1) * seq]
            sout_ref[b, h] = (state_ref[b, h] * cdec[h]
                              + ds[h * HEAD_DIM:(h + 1) * HEAD_DIM])
        return carry

    lax.fori_loop(0, bb, body, 0, unroll=True)


def _mix_sample(z, state, wt, bias_t, ln_g, ln_b, batch, seq):
    bb = 8
    cos2, sin2 = _rotary_tables(PAST_LEN + np.arange(seq))
    dmask, crossb, sdecb, cdec = _decay_tables(seq)
    dm = np.zeros((HEADS * seq, HEADS * seq), np.float32)
    for h in range(HEADS):
        dm[h * seq:(h + 1) * seq, h * seq:(h + 1) * seq] = dmask[h]
    cross_rows = crossb.reshape(HEADS * seq, HEAD_DIM)
    sdec_cols = np.ascontiguousarray(np.transpose(sdecb, (1, 0, 2)).reshape(seq, RET_WIDTH))
    const3 = lambda i: (0, 0, 0)
    const2 = lambda i: (0, 0)
    kern = functools.partial(_mix_sample_kernel, cdec=cdec, seq=seq)
    return pl.pallas_call(
        kern,
        grid=(batch // bb,),
        in_specs=[
            pl.BlockSpec((bb, seq, IN_WIDTH), lambda i: (i, 0, 0)),
            pl.BlockSpec((bb, HEADS, HEAD_DIM, HEAD_DIM), lambda i: (i, 0, 0, 0)),
            pl.BlockSpec((seq, HEAD_DIM), const2),
            pl.BlockSpec((seq, HEAD_DIM), const2),
            pl.BlockSpec((HEADS * seq, HEADS * seq), const2),
            pl.BlockSpec((HEADS * seq, HEAD_DIM), const2),
            pl.BlockSpec((seq, RET_WIDTH), const2),
            pl.BlockSpec((seq, seq, SGU_WIDTH), const3),
            pl.BlockSpec((seq, SGU_WIDTH), const2),
            pl.BlockSpec((1, SGU_WIDTH), const2),
            pl.BlockSpec((1, SGU_WIDTH), const2),
        ],
        out_specs=[
            pl.BlockSpec((bb, seq, D_MODEL), lambda i: (i, 0, 0)),
            pl.BlockSpec((bb, HEADS, HEAD_DIM, HEAD_DIM), lambda i: (i, 0, 0, 0)),
            pl.BlockSpec((bb, seq, SGU_WIDTH), lambda i: (i, 0, 0)),
        ],
        out_shape=[
            jax.ShapeDtypeStruct((batch, seq, D_MODEL), F32),
            jax.ShapeDtypeStruct((batch, HEADS, HEAD_DIM, HEAD_DIM), F32),
            jax.ShapeDtypeStruct((batch, seq, SGU_WIDTH), F32),
        ],
        compiler_params=pltpu.CompilerParams(
            dimension_semantics=("arbitrary",), vmem_limit_bytes=VMEM_LIMIT_SMALL),
        name="mix_sample",
    )(z.reshape(batch, seq, IN_WIDTH), state, jnp.asarray(cos2), jnp.asarray(sin2), jnp.asarray(dm),
      jnp.asarray(cross_rows), jnp.asarray(sdec_cols), wt, bias_t, ln_g, ln_b)


OUT_ROWS = 256


def _out_proj_kernel(m_ref, w_ref, x_ref, gpost_ref, gt_ref, gpre_ref, sc_ref, sh_ref,
                     x1_ref, h2_ref, *, mod_rows):
    tm = x_ref.shape[0]
    for r in range(tm // OUT_ROWS):
        rows = slice(r * OUT_ROWS, (r + 1) * OUT_ROWS)
        mrows = rows if mod_rows else slice(None)
        m = jnp.dot(m_ref[rows, :].astype(BF16), w_ref[...], preferred_element_type=F32)
        x1 = x_ref[rows, :] + gt_ref[mrows, :] * (m * _rms_scale(m) * gpost_ref[...])
        x1_ref[rows, :] = x1
        h2 = x1 * _rms_scale(x1) * gpre_ref[...] * (1.0 + sc_ref[mrows, :]) + sh_ref[mrows, :]
        h2_ref[rows, :] = h2.astype(BF16)


def _out_proj(m, w, x, gpost, gt, gpre, sc, sh, *, tm, tiles_per_mod):
    t = x.shape[0]
    mod_idx = lambda i: (i // tiles_per_mod, 0, 0)
    vec = pl.BlockSpec((1, D_MODEL), lambda i: (0, 0))
    kern = functools.partial(_out_proj_kernel, mod_rows=gt.shape[1] != 1)
    return pl.pallas_call(
        kern,
        grid=(t // tm,),
        in_specs=[
            pl.BlockSpec((tm, D_MODEL), lambda i: (i, 0)),
            pl.BlockSpec((D_MODEL, D_MODEL), lambda i: (0, 0)),
            pl.BlockSpec((tm, D_MODEL), lambda i: (i, 0)),
            vec,
            _mod_spec(gt, tm, mod_idx),
            vec,
            _mod_spec(sc, tm, mod_idx),
            _mod_spec(sh, tm, mod_idx),
        ],
        out_specs=[
            pl.BlockSpec((tm, D_MODEL), lambda i: (i, 0)),
            pl.BlockSpec((tm, D_MODEL), lambda i: (i, 0)),
        ],
        out_shape=[
            jax.ShapeDtypeStruct((t, D_MODEL), F32),
            jax.ShapeDtypeStruct((t, D_MODEL), BF16),
        ],
        compiler_params=pltpu.CompilerParams(
            dimension_semantics=("arbitrary",), vmem_limit_bytes=VMEM_LIMIT_BIG),
        name="out_proj",
    )(m, w, x, gpost, gt, gpre, sc, sh)


def _ffn_kernel(h2_ref, w1_ref, w2_ref, x1_ref, g_ref, gt_ref, y_ref):
    k = pl.program_id(1)

    @pl.when(k == 0)
    def _():
        y_ref[...] = jnp.zeros_like(y_ref)

    a = jnp.dot(h2_ref[...], w1_ref[...], preferred_element_type=F32)
    a = jnp.square(jnp.maximum(a, 0.0)).astype(BF16)
    y_ref[...] += jnp.dot(a, w2_ref[...], preferred_element_type=F32)

    @pl.when(k == pl.num_programs(1) - 1)
    def _():
        f = y_ref[...]
        y_ref[...] = x1_ref[...] + gt_ref[...] * (f * _rms_scale(f) * g_ref[...])


def _ffn(h2, w1, w2, x1, g, gt, *, tm, tiles_per_mod):
    t = h2.shape[0]
    tc = 1024
    mod_idx = lambda i, k: (i // tiles_per_mod, 0, 0)
    return pl.pallas_call(
        _ffn_kernel,
        grid=(t // tm, D_FF // tc),
        in_specs=[
            pl.BlockSpec((tm, D_MODEL), lambda i, k: (i, 0)),
            pl.BlockSpec((D_MODEL, tc), lambda i, k: (0, k)),
            pl.BlockSpec((tc, D_MODEL), lambda i, k: (k, 0)),
            pl.BlockSpec((tm, D_MODEL), lambda i, k: (i, 0)),
            pl.BlockSpec((1, D_MODEL), lambda i, k: (0, 0)),
            _mod_spec(gt, tm, mod_idx),
        ],
        out_specs=pl.BlockSpec((tm, D_MODEL), lambda i, k: (i, 0)),
        out_shape=jax.ShapeDtypeStruct((t, D_MODEL), F32),
        compiler_params=pltpu.CompilerParams(
            dimension_semantics=("arbitrary", "arbitrary"), vmem_limit_bytes=VMEM_LIMIT_BIG),
        name="ffn",
    )(h2, w1, w2, x1, g, gt)


TM_IN = 1024
TM = 512


def _after_in_proj(x, z, mods, state, weights, *, batch, seq, per_token_mod):
    (g_post_mix, g_pre_ffn, g_post_ffn, w_s, bias_tc, wt, bias_t, ln_g, ln_b, w_o, w_ff1, w_ff2) = weights
    gt1, sh2, sc2, gt2 = mods
    t = x.shape[0]
    tm = TM
    if per_token_mod:
        per = lambda tile: 1
    else:
        per = lambda tile: seq // tile

    def shaped(mod, tile):
        return mod.reshape(t // tile, tile, D_MODEL) if per_token_mod else mod

    if state is None:
        mix, s_new = _mix_prompt(z, w_s, bias_tc, ln_g, ln_b, batch, seq)
        vn = None
    else:
        mix, s_new, vn = _mix_sample(z, state, wt, bias_t, ln_g, ln_b, batch, seq)
    tm_out = OUT_ROWS if per_token_mod else tm
    x1, h2 = _out_proj(mix.reshape(t, D_MODEL), w_o, x, g_post_mix, shaped(gt1, tm_out), g_pre_ffn,
                       shaped(sc2, tm_out), shaped(sh2, tm_out), tm=tm_out, tiles_per_mod=per(tm_out))
    y = _ffn(h2, w_ff1, w_ff2, x1, g_post_ffn, shaped(gt2, tm), tm=tm, tiles_per_mod=per(tm))
    return y, s_new, vn


def kernel(x_prompt, x_sample, state_ret, c_prompt, c_sample, w_ada, b_ada, g_pre_mix, g_post_mix,
           g_pre_ffn, g_post_ffn, w_in, w_s, b_s, ln_g, ln_b, w_o, w_ff1, w_ff2):
    depth = w_ada.shape[0]
    batch, seq, _ = x_prompt.shape
    dec_batch, dec_seq, _ = x_sample.shape
    yp = x_prompt.reshape(batch * seq, D_MODEL)
    ys = x_sample.reshape(dec_batch * dec_seq, D_MODEL)
    c_all = jnp.concatenate([c_prompt, c_sample], axis=0)
    sp_list, ss_list, vs_list = [], [], []
    for l in range(depth):
        mod = _ada(c_all, w_ada[l], b_ada[l])
        mods_p = [m.reshape(batch, 1, D_MODEL) for m in jnp.split(mod[:batch], 6, axis=-1)]
        mods_s = [jnp.repeat(m, dec_seq, axis=0) for m in jnp.split(mod[batch:], 6, axis=-1)]
        bias_tc = jnp.repeat(b_s[l].T, CHUNK, axis=1)
        wt = jnp.repeat(jnp.transpose(w_s[l][:, :dec_seq, :dec_seq], (2, 1, 0)), CHUNK, axis=2)
        bias_t = bias_tc[:dec_seq]
        row = lambda v: v.reshape(1, -1)
        sh1_p, sc1_p = mods_p[0], mods_p[1]
        sh1_s, sc1_s = (m.reshape(1, dec_batch * dec_seq, D_MODEL) for m in mods_s[:2])
        zs, w_in_bf = _norm_matmul_cast(ys, row(g_pre_mix[l]), sc1_s, sh1_s, w_in[l], tiles_per_mod=1)
        zp, w_o_bf, w_ff1_bf, w_ff2_bf = _norm_matmul_side(
            yp, row(g_pre_mix[l]), sc1_p, sh1_p, w_in_bf, (w_o[l], w_ff1[l], w_ff2[l]),
            tm=TM_IN, tiles_per_mod=seq // TM_IN)
        weights = (row(g_post_mix[l]), row(g_pre_ffn[l]), row(g_post_ffn[l]), w_s[l], bias_tc, wt,
                   bias_t, row(ln_g[l]), row(ln_b[l]), w_o_bf, w_ff1_bf, w_ff2_bf)
        yp, sp, _ = _after_in_proj(yp, zp, mods_p[2:], None, weights, batch=batch, seq=seq,
                                   per_token_mod=False)
        ys, ss, vn = _after_in_proj(ys, zs, mods_s[2:], state_ret[l], weights, batch=dec_batch,
                                    seq=dec_seq, per_token_mod=True)
        sp_list.append(sp)
        ss_list.append(ss)
        vs_list.append(vn)
    return (yp.reshape(batch, seq, D_MODEL), ys.reshape(dec_batch, dec_seq, D_MODEL),
            jnp.stack(sp_list), jnp.stack(ss_list), jnp.stack(vs_list))
```

```python
import functools

import numpy as np
import jax
import jax.numpy as jnp
from jax import lax
from jax.experimental import pallas as pl
from jax.experimental.pallas import tpu as pltpu

D_MODEL = 2048
RET_WIDTH = 1024
SGU_WIDTH = 1024
HEADS = 8
HEAD_DIM = 128
GROUPS = 8
CHUNK = 128
D_FF = 4 * D_MODEL
IN_WIDTH = 4 * RET_WIDTH + 2 * SGU_WIDTH
ROPE_THETA = 10000.0
EPS = 1e-6
PAST_LEN = 16384

F32 = jnp.float32
BF16 = jnp.bfloat16

VMEM_LIMIT_BIG = 58 * 1024 * 1024
VMEM_LIMIT_SMALL = 40 * 1024 * 1024


def _decay_tables(clen):
    lg = np.log(1.0 - np.power(2.0, -5.0 - np.arange(HEADS, dtype=np.float64)))
    idx = np.arange(clen, dtype=np.float64)
    diff = idx[:, None] - idx[None, :]
    dmask = np.where(diff[None] >= 0, np.exp(np.maximum(diff, 0.0)[None] * lg[:, None, None]), 0.0)
    cross = np.exp((idx + 1.0)[None, :] * lg[:, None])
    sdec = np.exp((clen - 1.0 - idx)[None, :] * lg[:, None])
    cdec = np.exp(clen * lg)
    crossb = np.broadcast_to(cross[:, :, None], (HEADS, clen, HEAD_DIM))
    sdecb = np.broadcast_to(sdec[:, :, None], (HEADS, clen, HEAD_DIM))
    return (dmask.astype(np.float32), np.ascontiguousarray(crossb, dtype=np.float32),
            np.ascontiguousarray(sdecb, dtype=np.float32), [float(c) for c in cdec])


def _rotary_tables(pos, scale=1.0):
    inv = 1.0 / (ROPE_THETA ** (np.arange(0, HEAD_DIM, 2, dtype=np.float64) / HEAD_DIM))
    ang = np.asarray(pos, dtype=np.float64)[:, None] * inv[None, :]
    cos, sin = np.cos(ang) * scale, np.sin(ang) * scale
    cos2 = np.concatenate([cos, cos], axis=-1).astype(np.float32)
    sin2 = np.concatenate([-sin, sin], axis=-1).astype(np.float32)
    return cos2, sin2


def _rms_scale(x):
    return lax.rsqrt(jnp.mean(x * x, axis=-1, keepdims=True) + EPS)


def _layernorm(x):
    mu = jnp.mean(x, axis=-1, keepdims=True)
    xc = x - mu
    var = jnp.mean(xc * xc, axis=-1, keepdims=True)
    return xc * lax.rsqrt(var + EPS)


LOG2E = 1.4426950408889634
GELU_C1 = -2.0 * np.sqrt(2.0 / np.pi) * LOG2E
GELU_C3 = GELU_C1 * 0.044715


def _gelu(x):
    return x / (1.0 + jnp.exp2(x * (GELU_C1 + GELU_C3 * (x * x))))


def _silu_gate(g, y):
    return (g * y) / (1.0 + jnp.exp2(g * (-LOG2E)))


def _rotary(x, cos2, sin2):
    return x * cos2 + pltpu.roll(x, HEAD_DIM // 2, 1) * sin2


def _ada_kernel(c_ref, w_ref, b_ref, o_ref):
    c = c_ref[...]
    s = (c * jax.nn.sigmoid(c)).astype(BF16)
    o_ref[...] = jnp.dot(s, w_ref[...].astype(BF16), preferred_element_type=F32) + b_ref[...]


def _ada(c_all, w_ada, b_ada):
    n_rows = c_all.shape[0]
    tn = 1024
    n_out = w_ada.shape[1]
    return pl.pallas_call(
        _ada_kernel,
        grid=(n_out // tn,),
        in_specs=[
            pl.BlockSpec((n_rows, D_MODEL), lambda j: (0, 0)),
            pl.BlockSpec((D_MODEL, tn), lambda j: (0, j)),
            pl.BlockSpec((1, tn), lambda j: (0, j)),
        ],
        out_specs=pl.BlockSpec((n_rows, tn), lambda j: (0, j)),
        out_shape=jax.ShapeDtypeStruct((n_rows, n_out), F32),
        compiler_params=pltpu.CompilerParams(
            dimension_semantics=("arbitrary",), vmem_limit_bytes=VMEM_LIMIT_SMALL),
        name="ada_modulation",
    )(c_all, w_ada, b_ada.reshape(1, n_out))


NORM_ROWS = 256


def _mod_rows(ref, rows, tm):
    n_mod, width = ref.shape
    if n_mod == 1:
        return ref[...]
    rep = tm // n_mod
    n = (rows.stop - rows.start) // rep
    a = ref[rows.start // rep:rows.start // rep + n, :]
    return jnp.broadcast_to(a[:, None, :], (n, rep, width)).reshape(n * rep, width)


def _norm_matmul_steps(x_ref, g_ref, sc_ref, sh_ref, w_ref, z_ref, h_ref):
    tm = x_ref.shape[0]

    @pl.when(pl.program_id(1) == 0)
    def _():
        g = g_ref[...]
        for r in range(tm // NORM_ROWS):
            rows = slice(r * NORM_ROWS, (r + 1) * NORM_ROWS)
            x = x_ref[rows, :]
            h = (x * _rms_scale(x) * g * (1.0 + _mod_rows(sc_ref, rows, tm))
                 + _mod_rows(sh_ref, rows, tm)).astype(BF16)
            h_ref[rows, :] = h
            z_ref[rows, :] = jnp.dot(h, w_ref[...], preferred_element_type=F32)

    @pl.when(pl.program_id(1) != 0)
    def _():
        z_ref[...] = jnp.dot(h_ref[...], w_ref[...], preferred_element_type=F32)


def _norm_matmul_cast_kernel(x_ref, g_ref, sc_ref, sh_ref, w32_ref, z_ref, wbf_ref, h_ref):
    wbf_ref[...] = w32_ref[...].astype(BF16)
    _norm_matmul_steps(x_ref, g_ref, sc_ref, sh_ref, wbf_ref, z_ref, h_ref)


def _norm_matmul_side_kernel(x_ref, g_ref, sc_ref, sh_ref, w_ref, a32_ref, b32_ref, c32_ref,
                             z_ref, abf_ref, bbf_ref, cbf_ref, h_ref, *, side_steps):
    @pl.when(pl.program_id(1) < side_steps)
    def _():
        abf_ref[...] = a32_ref[...].astype(BF16)
        bbf_ref[...] = b32_ref[...].astype(BF16)
        cbf_ref[...] = c32_ref[...].astype(BF16)

    _norm_matmul_steps(x_ref, g_ref, sc_ref, sh_ref, w_ref, z_ref, h_ref)


def _mod_spec(mod, tm, index_of_tile):
    rows = mod.shape[1]
    return pl.BlockSpec((None, rows, D_MODEL), index_of_tile)


def _norm_matmul_specs(sc, sh, tm, tn, tiles_per_mod):
    mod_idx = lambda i, j: (i // tiles_per_mod, 0, 0)
    return [
        pl.BlockSpec((tm, D_MODEL), lambda i, j: (i, 0)),
        pl.BlockSpec((1, D_MODEL), lambda i, j: (0, 0)),
        _mod_spec(sc, tm, mod_idx),
        _mod_spec(sh, tm, mod_idx),
        pl.BlockSpec((D_MODEL, tn), lambda i, j: (0, j)),
    ]


def _norm_matmul_cast(x, g, sc, sh, w32, *, tiles_per_mod):
    t = x.shape[0]
    tn = 512
    n = w32.shape[1]
    return pl.pallas_call(
        _norm_matmul_cast_kernel,
        grid=(1, n // tn),
        in_specs=_norm_matmul_specs(sc, sh, t, tn, tiles_per_mod),
        out_specs=[
            pl.BlockSpec((t, tn), lambda i, j: (i, j)),
            pl.BlockSpec((D_MODEL, tn), lambda i, j: (0, j)),
        ],
        out_shape=[
            jax.ShapeDtypeStruct((t, n), F32),
            jax.ShapeDtypeStruct(w32.shape, BF16),
        ],
        scratch_shapes=[pltpu.VMEM((t, D_MODEL), BF16)],
        compiler_params=pltpu.CompilerParams(
            dimension_semantics=("arbitrary", "arbitrary"), vmem_limit_bytes=VMEM_LIMIT_BIG),
        name="norm_in_proj_cast",
    )(x, g, sc, sh, w32)


SIDE_STEPS = 4


def _norm_matmul_side(x, g, sc, sh, w, side32, *, tm, tiles_per_mod):
    t = x.shape[0]
    tn = 1024
    n = w.shape[1]
    n_slabs = (t // tm) * SIDE_STEPS
    slab_idx = lambda i, j: (i * SIDE_STEPS + jnp.minimum(j, SIDE_STEPS - 1), 0)
    side_specs = [pl.BlockSpec((a.shape[0] // n_slabs, a.shape[1]), slab_idx) for a in side32]
    kern = functools.partial(_norm_matmul_side_kernel, side_steps=SIDE_STEPS)
    return pl.pallas_call(
        kern,
        grid=(t // tm, n // tn),
        in_specs=_norm_matmul_specs(sc, sh, tm, tn, tiles_per_mod) + side_specs,
        out_specs=[pl.BlockSpec((tm, tn), lambda i, j: (i, j))] + side_specs,
        out_shape=[jax.ShapeDtypeStruct((t, n), F32)]
                  + [jax.ShapeDtypeStruct(a.shape, BF16) for a in side32],
        scratch_shapes=[pltpu.VMEM((tm, D_MODEL), BF16)],
        compiler_params=pltpu.CompilerParams(
            dimension_semantics=("arbitrary", "arbitrary"), vmem_limit_bytes=VMEM_LIMIT_BIG),
        name="norm_in_proj",
    )(x, g, sc, sh, w, *side32)


def _mix_prompt_kernel(z_ref, rot_ref, dmask_ref, cross_ref, sdec_ref, ws_ref, bias_ref,
                       lng_ref, lnb_ref, o_ref, sout_ref, s_ref, *, cdec):
    c = pl.program_id(1)

    @pl.when(c == 0)
    def _():
        s_ref[...] = jnp.zeros_like(s_ref)

    row = lax.broadcasted_iota(jnp.int32, (CHUNK, CHUNK), 0)
    col = lax.broadcasted_iota(jnp.int32, (CHUNK, CHUNK), 1)
    causal = row >= col
    for ci in range(z_ref.shape[0] // CHUNK):
        rows = slice(ci * CHUNK, (ci + 1) * CHUNK)
        cos_q, sin_q, cos_k, sin_k = (rot_ref[rows, i * HEAD_DIM:(i + 1) * HEAD_DIM] for i in range(4))
        for h in range(HEADS):
            lo = h * HEAD_DIM
            q = _rotary(z_ref[rows, lo:lo + HEAD_DIM], cos_q, sin_q)
            k = _rotary(z_ref[rows, RET_WIDTH + lo:RET_WIDTH + lo + HEAD_DIM], cos_k, sin_k)
            v = z_ref[rows, 2 * RET_WIDTH + lo:2 * RET_WIDTH + lo + HEAD_DIM].astype(BF16)
            g = z_ref[rows, 3 * RET_WIDTH + lo:3 * RET_WIDTH + lo + HEAD_DIM]
            qb = q.astype(BF16)
            kb = k.astype(BF16)
            scores = lax.dot_general(qb, kb, (((1,), (1,)), ((), ())),
                                     preferred_element_type=F32) * dmask_ref[h]
            state = s_ref[h]
            o = (jnp.dot(scores.astype(BF16), v, preferred_element_type=F32)
                 + jnp.dot(qb, state.astype(BF16), preferred_element_type=F32) * cross_ref[h])
            kd = (k * sdec_ref[h]).astype(BF16)
            s_ref[h] = state * cdec[h] + lax.dot_general(kd, v, (((0,), (0,)), ((), ())),
                                                         preferred_element_type=F32)
            o_ref[rows, lo:lo + HEAD_DIM] = _silu_gate(g, _layernorm(o)).astype(o_ref.dtype)

        u = _gelu(z_ref[rows, 4 * RET_WIDTH:4 * RET_WIDTH + SGU_WIDTH])
        vn = (_layernorm(_gelu(z_ref[rows, 4 * RET_WIDTH + SGU_WIDTH:])) * lng_ref[...]
              + lnb_ref[...])
        for gi in range(GROUPS):
            lo = gi * CHUNK
            w = jnp.where(causal, ws_ref[gi], 0.0).astype(BF16)
            s = (jnp.dot(w, vn[:, lo:lo + CHUNK].astype(BF16), preferred_element_type=F32)
                 + bias_ref[:, lo:lo + CHUNK])
            o_ref[rows, RET_WIDTH + lo:RET_WIDTH + lo + CHUNK] = (u[:, lo:lo + CHUNK] * s).astype(o_ref.dtype)

    @pl.when(c == pl.num_programs(1) - 1)
    def _():
        sout_ref[...] = s_ref[...]


MIX_CHUNKS_PER_STEP = 4


def _mix_prompt(z, w_s, bias_tc, ln_g, ln_b, batch, seq):
    rows = MIX_CHUNKS_PER_STEP * CHUNK
    rot = np.concatenate(_rotary_tables(np.arange(seq)) + _rotary_tables(np.arange(seq), HEAD_DIM ** -0.5),
                         axis=1)
    dmask, crossb, sdecb, cdec = _decay_tables(CHUNK)
    const3 = lambda b, c: (0, 0, 0)
    const2 = lambda b, c: (0, 0)
    kern = functools.partial(_mix_prompt_kernel, cdec=cdec)
    return pl.pallas_call(
        kern,
        grid=(batch, seq // rows),
        in_specs=[
            pl.BlockSpec((None, rows, IN_WIDTH), lambda b, c: (b, c, 0)),
            pl.BlockSpec((rows, 4 * HEAD_DIM), lambda b, c: (c, 0)),
            pl.BlockSpec((HEADS, CHUNK, CHUNK), const3),
            pl.BlockSpec((HEADS, CHUNK, HEAD_DIM), const3),
            pl.BlockSpec((HEADS, CHUNK, HEAD_DIM), const3),
            pl.BlockSpec((GROUPS, CHUNK, CHUNK), const3),
            pl.BlockSpec((CHUNK, SGU_WIDTH), const2),
            pl.BlockSpec((1, SGU_WIDTH), const2),
            pl.BlockSpec((1, SGU_WIDTH), const2),
        ],
        out_specs=[
            pl.BlockSpec((None, rows, D_MODEL), lambda b, c: (b, c, 0)),
            pl.BlockSpec((None, HEADS, HEAD_DIM, HEAD_DIM), lambda b, c: (b, 0, 0, 0)),
        ],
        out_shape=[
            jax.ShapeDtypeStruct((batch, seq, D_MODEL), BF16),
            jax.ShapeDtypeStruct((batch, HEADS, HEAD_DIM, HEAD_DIM), F32),
        ],
        scratch_shapes=[pltpu.VMEM((HEADS, HEAD_DIM, HEAD_DIM), F32)],
        compiler_params=pltpu.CompilerParams(
            dimension_semantics=("arbitrary", "arbitrary"), vmem_limit_bytes=VMEM_LIMIT_SMALL),
        name="mix_prompt",
    )(z.reshape(batch, seq, IN_WIDTH), jnp.asarray(rot), jnp.asarray(dmask),
      jnp.asarray(crossb), jnp.asarray(sdecb), w_s, bias_tc, ln_g, ln_b)


def _block_diag(x, mask):
    return jnp.where(mask, jnp.concatenate([x] * HEADS, axis=0), 0.0)


def _heads_to_rows(x):
    return jnp.concatenate([x[:, h * HEAD_DIM:(h + 1) * HEAD_DIM] for h in range(HEADS)], axis=0)


def _mix_sample_kernel(z_ref, state_ref, cos_ref, sin_ref, dmask_ref, cross_ref, sdec_ref, wt_ref,
                       bias_ref, lng_ref, lnb_ref, o_ref, sout_ref, vn_ref, *, cdec, seq):
    bb = z_ref.shape[0]

    u = _gelu(z_ref[:, :, 4 * RET_WIDTH:4 * RET_WIDTH + SGU_WIDTH])
    vn = _layernorm(_gelu(z_ref[:, :, 4 * RET_WIDTH + SGU_WIDTH:])) * lng_ref[...] + lnb_ref[...]
    vn_ref[...] = vn
    trow = lax.broadcasted_iota(jnp.int32, (seq, SGU_WIDTH), 0)
    s = jnp.broadcast_to(bias_ref[...], (bb, seq, SGU_WIDTH))
    for src in range(seq):
        w = jnp.where(trow >= src, wt_ref[src], 0.0)
        s = s + w * vn[:, src:src + 1, :]
    o_ref[:, :, RET_WIDTH:] = u * s

    cos2 = cos_ref[...]
    sin2 = sin_ref[...]
    rblk = lax.broadcasted_iota(jnp.int32, (HEADS * seq, RET_WIDTH), 0) // seq
    cblk = lax.broadcasted_iota(jnp.int32, (HEADS * seq, RET_WIDTH), 1) // HEAD_DIM
    diag = rblk == cblk

    def rotary_heads(base, b):
        return jnp.concatenate(
            [_rotary(z_ref[b, :, base + h * HEAD_DIM:base + (h + 1) * HEAD_DIM], cos2, sin2)
             for h in range(HEADS)], axis=1)

    def body(b, carry):
        q = rotary_heads(0, b)
        k = rotary_heads(RET_WIDTH, b) * (HEAD_DIM ** -0.5)
        v_rows = _heads_to_rows(z_ref[b, :, 2 * RET_WIDTH:3 * RET_WIDTH]).astype(BF16)
        g_rows = _heads_to_rows(z_ref[b, :, 3 * RET_WIDTH:4 * RET_WIDTH])
        qbd = _block_diag(q, diag).astype(BF16)
        kbd = _block_diag(k, diag).astype(BF16)
        kdbd = _block_diag(k * sdec_ref[...], diag).astype(BF16)
        scores = lax.dot_general(qbd, kbd, (((1,), (1,)), ((), ())),
                                 preferred_element_type=F32) * dmask_ref[...]
        state = state_ref[b].reshape(HEADS * HEAD_DIM, HEAD_DIM)
        o = (jnp.dot(scores.astype(BF16), v_rows, preferred_element_type=F32)
             + jnp.dot(qbd, state.astype(BF16), preferred_element_type=F32) * cross_ref[...])
        out = _silu_gate(g_rows, _layernorm(o))
        ds = lax.dot_general(kdbd, v_rows, (((0,), (0,)), ((), ())), preferred_element_type=F32)
        for h in range(HEADS):
            o_ref[b, :, h * HEAD_DIM:(h + 1) * HEAD_DIM] = out[h * seq:(h + 1) * seq]
            sout_ref[b, h] = (state_ref[b, h] * cdec[h]
                              + ds[h * HEAD_DIM:(h + 1) * HEAD_DIM])
        return carry

    lax.fori_loop(0, bb, body, 0, unroll=True)


def _mix_sample(z, state, wt, bias_t, ln_g, ln_b, batch, seq):
    bb = 8
    cos2, sin2 = _rotary_tables(PAST_LEN + np.arange(seq))
    dmask, crossb, sdecb, cdec = _decay_tables(seq)
    dm = np.zeros((HEADS * seq, HEADS * seq), np.float32)
    for h in range(HEADS):
        dm[h * seq:(h + 1) * seq, h * seq:(h + 1) * seq] = dmask[h]
    cross_rows = crossb.reshape(HEADS * seq, HEAD_DIM)
    sdec_cols = np.ascontiguousarray(np.transpose(sdecb, (1, 0, 2)).reshape(seq, RET_WIDTH))
    const3 = lambda i: (0, 0, 0)
    const2 = lambda i: (0, 0)
    kern = functools.partial(_mix_sample_kernel, cdec=cdec, seq=seq)
    return pl.pallas_call(
        kern,
        grid=(batch // bb,),
        in_specs=[
            pl.BlockSpec((bb, seq, IN_WIDTH), lambda i: (i, 0, 0)),
            pl.BlockSpec((bb, HEADS, HEAD_DIM, HEAD_DIM), lambda i: (i, 0, 0, 0)),
            pl.BlockSpec((seq, HEAD_DIM), const2),
            pl.BlockSpec((seq, HEAD_DIM), const2),
            pl.BlockSpec((HEADS * seq, HEADS * seq), const2),
            pl.BlockSpec((HEADS * seq, HEAD_DIM), const2),
            pl.BlockSpec((seq, RET_WIDTH), const2),
            pl.BlockSpec((seq, seq, SGU_WIDTH), const3),
            pl.BlockSpec((seq, SGU_WIDTH), const2),
            pl.BlockSpec((1, SGU_WIDTH), const2),
            pl.BlockSpec((1, SGU_WIDTH), const2),
        ],
        out_specs=[
            pl.BlockSpec((bb, seq, D_MODEL), lambda i: (i, 0, 0)),
            pl.BlockSpec((bb, HEADS, HEAD_DIM, HEAD_DIM), lambda i: (i, 0, 0, 0)),
            pl.BlockSpec((bb, seq, SGU_WIDTH), lambda i: (i, 0, 0)),
        ],
        out_shape=[
            jax.ShapeDtypeStruct((batch, seq, D_MODEL), F32),
            jax.ShapeDtypeStruct((batch, HEADS, HEAD_DIM, HEAD_DIM), F32),
            jax.ShapeDtypeStruct((batch, seq, SGU_WIDTH), F32),
        ],
        compiler_params=pltpu.CompilerParams(
            dimension_semantics=("arbitrary",), vmem_limit_bytes=VMEM_LIMIT_SMALL),
        name="mix_sample",
    )(z.reshape(batch, seq, IN_WIDTH), state, jnp.asarray(cos2), jnp.asarray(sin2), jnp.asarray(dm),
      jnp.asarray(cross_rows), jnp.asarray(sdec_cols), wt, bias_t, ln_g, ln_b)


OUT_ROWS = 256


def _out_proj_kernel(m_ref, w_ref, x_ref, gpost_ref, gt_ref, gpre_ref, sc_ref, sh_ref,
                     x1_ref, h2_ref):
    tm = x_ref.shape[0]
    for r in range(tm // OUT_ROWS):
        rows = slice(r * OUT_ROWS, (r + 1) * OUT_ROWS)
        m = jnp.dot(m_ref[rows, :].astype(BF16), w_ref[...], preferred_element_type=F32)
        x1 = x_ref[rows, :] + _mod_rows(gt_ref, rows, tm) * (m * _rms_scale(m) * gpost_ref[...])
        x1_ref[rows, :] = x1
        h2 = (x1 * _rms_scale(x1) * gpre_ref[...] * (1.0 + _mod_rows(sc_ref, rows, tm))
              + _mod_rows(sh_ref, rows, tm))
        h2_ref[rows, :] = h2.astype(BF16)


def _out_proj(m, w, x, gpost, gt, gpre, sc, sh, *, tm, tiles_per_mod):
    t = x.shape[0]
    mod_idx = lambda i: (i // tiles_per_mod, 0, 0)
    vec = pl.BlockSpec((1, D_MODEL), lambda i: (0, 0))
    return pl.pallas_call(
        _out_proj_kernel,
        grid=(t // tm,),
        in_specs=[
            pl.BlockSpec((tm, D_MODEL), lambda i: (i, 0)),
            pl.BlockSpec((D_MODEL, D_MODEL), lambda i: (0, 0)),
            pl.BlockSpec((tm, D_MODEL), lambda i: (i, 0)),
            vec,
            _mod_spec(gt, tm, mod_idx),
            vec,
            _mod_spec(sc, tm, mod_idx),
            _mod_spec(sh, tm, mod_idx),
        ],
        out_specs=[
            pl.BlockSpec((tm, D_MODEL), lambda i: (i, 0)),
            pl.BlockSpec((tm, D_MODEL), lambda i: (i, 0)),
        ],
        out_shape=[
            jax.ShapeDtypeStruct((t, D_MODEL), F32),
            jax.ShapeDtypeStruct((t, D_MODEL), BF16),
        ],
        compiler_params=pltpu.CompilerParams(
            dimension_semantics=("arbitrary",), vmem_limit_bytes=VMEM_LIMIT_BIG),
        name="out_proj",
    )(m, w, x, gpost, gt, gpre, sc, sh)


FFN_ROWS = 512


def _ffn_kernel(h2_ref, w1_ref, w2_ref, x1_ref, g_ref, gt_ref, y_ref):
    k = pl.program_id(1)
    tm = y_ref.shape[0]
    chunks = [slice(r * FFN_ROWS, (r + 1) * FFN_ROWS) for r in range(tm // FFN_ROWS)]

    @pl.when(k == 0)
    def _():
        y_ref[...] = jnp.zeros_like(y_ref)

    for rows in chunks:
        a = jnp.dot(h2_ref[rows, :], w1_ref[...], preferred_element_type=F32)
        a = jnp.square(jnp.maximum(a, 0.0)).astype(BF16)
        y_ref[rows, :] += jnp.dot(a, w2_ref[...], preferred_element_type=F32)

    @pl.when(k == pl.num_programs(1) - 1)
    def _():
        for rows in chunks:
            f = y_ref[rows, :]
            y_ref[rows, :] = x1_ref[rows, :] + _mod_rows(gt_ref, rows, tm) * (f * _rms_scale(f) * g_ref[...])


def _ffn(h2, w1, w2, x1, g, gt, *, tm, tiles_per_mod):
    t = h2.shape[0]
    tc = 1024
    mod_idx = lambda i, k: (i // tiles_per_mod, 0, 0)
    return pl.pallas_call(
        _ffn_kernel,
        grid=(t // tm, D_FF // tc),
        in_specs=[
            pl.BlockSpec((tm, D_MODEL), lambda i, k: (i, 0)),
            pl.BlockSpec((D_MODEL, tc), lambda i, k: (0, k)),
            pl.BlockSpec((tc, D_MODEL), lambda i, k: (k, 0)),
            pl.BlockSpec((tm, D_MODEL), lambda i, k: (i, 0)),
            pl.BlockSpec((1, D_MODEL), lambda i, k: (0, 0)),
            _mod_spec(gt, tm, mod_idx),
        ],
        out_specs=pl.BlockSpec((tm, D_MODEL), lambda i, k: (i, 0)),
        out_shape=jax.ShapeDtypeStruct((t, D_MODEL), F32),
        compiler_params=pltpu.CompilerParams(
            dimension_semantics=("arbitrary", "arbitrary"), vmem_limit_bytes=VMEM_LIMIT_BIG),
        name="ffn",
    )(h2, w1, w2, x1, g, gt)


TM_IN = 1024
TM = 512
TM_SHORT = 1024


def _after_in_proj(x, z, mods, state, weights, *, batch, seq, per_token_mod):
    (g_post_mix, g_pre_ffn, g_post_ffn, w_s, bias_tc, wt, bias_t, ln_g, ln_b, w_o, w_ff1, w_ff2) = weights
    gt1, sh2, sc2, gt2 = mods
    t = x.shape[0]
    tm = TM_SHORT if per_token_mod else TM
    if per_token_mod:
        per = lambda tile: 1
    else:
        per = lambda tile: seq // tile

    def shaped(mod, tile):
        return mod.reshape(t // tile, tile // seq, D_MODEL) if per_token_mod else mod.reshape(batch, 1, D_MODEL)

    if state is None:
        mix, s_new = _mix_prompt(z, w_s, bias_tc, ln_g, ln_b, batch, seq)
        vn = None
    else:
        mix, s_new, vn = _mix_sample(z, state, wt, bias_t, ln_g, ln_b, batch, seq)
    tm_out = tm
    x1, h2 = _out_proj(mix.reshape(t, D_MODEL), w_o, x, g_post_mix, shaped(gt1, tm_out), g_pre_ffn,
                       shaped(sc2, tm_out), shaped(sh2, tm_out), tm=tm_out, tiles_per_mod=per(tm_out))
    y = _ffn(h2, w_ff1, w_ff2, x1, g_post_ffn, shaped(gt2, tm), tm=tm, tiles_per_mod=per(tm))
    return y, s_new, vn


def kernel(x_prompt, x_sample, state_ret, c_prompt, c_sample, w_ada, b_ada, g_pre_mix, g_post_mix,
           g_pre_ffn, g_post_ffn, w_in, w_s, b_s, ln_g, ln_b, w_o, w_ff1, w_ff2):
    depth = w_ada.shape[0]
    batch, seq, _ = x_prompt.shape
    dec_batch, dec_seq, _ = x_sample.shape
    yp = x_prompt.reshape(batch * seq, D_MODEL)
    ys = x_sample.reshape(dec_batch * dec_seq, D_MODEL)
    c_all = jnp.concatenate([c_prompt, c_sample], axis=0)
    sp_list, ss_list, vs_list = [], [], []
    for l in range(depth):
        mod = _ada(c_all, w_ada[l], b_ada[l])
        mods_p = jnp.split(mod[:batch], 6, axis=-1)
        mods_s = jnp.split(mod[batch:], 6, axis=-1)
        bias_tc = jnp.repeat(b_s[l].T, CHUNK, axis=1)
        wt = jnp.repeat(jnp.transpose(w_s[l][:, :dec_seq, :dec_seq], (2, 1, 0)), CHUNK, axis=2)
        bias_t = bias_tc[:dec_seq]
        row = lambda v: v.reshape(1, -1)
        sh1_p, sc1_p = (m.reshape(batch, 1, D_MODEL) for m in mods_p[:2])
        sh1_s, sc1_s = (m.reshape(1, dec_batch, D_MODEL) for m in mods_s[:2])
        zs, w_in_bf = _norm_matmul_cast(ys, row(g_pre_mix[l]), sc1_s, sh1_s, w_in[l], tiles_per_mod=1)
        zp, w_o_bf, w_ff1_bf, w_ff2_bf = _norm_matmul_side(
            yp, row(g_pre_mix[l]), sc1_p, sh1_p, w_in_bf, (w_o[l], w_ff1[l], w_ff2[l]),
            tm=TM_IN, tiles_per_mod=seq // TM_IN)
        weights = (row(g_post_mix[l]), row(g_pre_ffn[l]), row(g_post_ffn[l]), w_s[l], bias_tc, wt,
                   bias_t, row(ln_g[l]), row(ln_b[l]), w_o_bf, w_ff1_bf, w_ff2_bf)
        yp, sp, _ = _after_in_proj(yp, zp, mods_p[2:], None, weights, batch=batch, seq=seq,
                                   per_token_mod=False)
        ys, ss, vn = _after_in_proj(ys, zs, mods_s[2:], state_ret[l], weights, batch=dec_batch,
                                    seq=dec_seq, per_token_mod=True)
        sp_list.append(sp)
        ss_list.append(ss)
        vs_list.append(vn)
    return (yp.reshape(batch, seq, D_MODEL), ys.reshape(dec_batch, dec_seq, D_MODEL),
            jnp.stack(sp_list), jnp.stack(ss_list), jnp.stack(vs_list))
```

```python
import functools

import numpy as np
import jax
import jax.numpy as jnp
from jax import lax
from jax.experimental import pallas as pl
from jax.experimental.pallas import tpu as pltpu

D_MODEL = 2048
RET_WIDTH = 1024
SGU_WIDTH = 1024
HEADS = 8
HEAD_DIM = 128
GROUPS = 8
CHUNK = 128
D_FF = 4 * D_MODEL
IN_WIDTH = 4 * RET_WIDTH + 2 * SGU_WIDTH
ROPE_THETA = 10000.0
EPS = 1e-6
PAST_LEN = 16384

F32 = jnp.float32
BF16 = jnp.bfloat16

VMEM_LIMIT_BIG = 58 * 1024 * 1024
VMEM_LIMIT_SMALL = 40 * 1024 * 1024


def _decay_tables(clen):
    lg = np.log(1.0 - np.power(2.0, -5.0 - np.arange(HEADS, dtype=np.float64)))
    idx = np.arange(clen, dtype=np.float64)
    diff = idx[:, None] - idx[None, :]
    dmask = np.where(diff[None] >= 0, np.exp(np.maximum(diff, 0.0)[None] * lg[:, None, None]), 0.0)
    cross = np.exp((idx + 1.0)[None, :] * lg[:, None])
    sdec = np.exp((clen - 1.0 - idx)[None, :] * lg[:, None])
    cdec = np.exp(clen * lg)
    crossb = np.broadcast_to(cross[:, :, None], (HEADS, clen, HEAD_DIM))
    sdecb = np.broadcast_to(sdec[:, :, None], (HEADS, clen, HEAD_DIM))
    return (dmask.astype(np.float32), np.ascontiguousarray(crossb, dtype=np.float32),
            np.ascontiguousarray(sdecb, dtype=np.float32), [float(c) for c in cdec])


def _rotary_tables(pos, scale=1.0):
    inv = 1.0 / (ROPE_THETA ** (np.arange(0, HEAD_DIM, 2, dtype=np.float64) / HEAD_DIM))
    ang = np.asarray(pos, dtype=np.float64)[:, None] * inv[None, :]
    cos, sin = np.cos(ang) * scale, np.sin(ang) * scale
    cos2 = np.concatenate([cos, cos], axis=-1).astype(np.float32)
    sin2 = np.concatenate([-sin, sin], axis=-1).astype(np.float32)
    return cos2, sin2


def _rms_scale(x):
    return lax.rsqrt(jnp.mean(x * x, axis=-1, keepdims=True) + EPS)


def _layernorm(x):
    mu = jnp.mean(x, axis=-1, keepdims=True)
    xc = x - mu
    var = jnp.mean(xc * xc, axis=-1, keepdims=True)
    return xc * lax.rsqrt(var + EPS)


LOG2E = 1.4426950408889634
GELU_C1 = -2.0 * np.sqrt(2.0 / np.pi) * LOG2E
GELU_C3 = GELU_C1 * 0.044715


def _gelu(x):
    return x / (1.0 + jnp.exp2(x * (GELU_C1 + GELU_C3 * (x * x))))


def _silu_gate(g, y):
    return (g * y) / (1.0 + jnp.exp2(g * (-LOG2E)))


def _rotary(x, cos2, sin2):
    return x * cos2 + pltpu.roll(x, HEAD_DIM // 2, 1) * sin2


def _ada_kernel(c_ref, w_ref, b_ref, o_ref):
    c = c_ref[...]
    s = (c * jax.nn.sigmoid(c)).astype(BF16)
    o_ref[...] = jnp.dot(s, w_ref[...].astype(BF16), preferred_element_type=F32) + b_ref[...]


def _ada(c_all, w_ada, b_ada):
    n_rows = c_all.shape[0]
    tn = 1024
    n_out = w_ada.shape[1]
    return pl.pallas_call(
        _ada_kernel,
        grid=(n_out // tn,),
        in_specs=[
            pl.BlockSpec((n_rows, D_MODEL), lambda j: (0, 0)),
            pl.BlockSpec((D_MODEL, tn), lambda j: (0, j)),
            pl.BlockSpec((1, tn), lambda j: (0, j)),
        ],
        out_specs=pl.BlockSpec((n_rows, tn), lambda j: (0, j)),
        out_shape=jax.ShapeDtypeStruct((n_rows, n_out), F32),
        compiler_params=pltpu.CompilerParams(
            dimension_semantics=("arbitrary",), vmem_limit_bytes=VMEM_LIMIT_SMALL),
        name="ada_modulation",
    )(c_all, w_ada, b_ada.reshape(1, n_out))


NORM_ROWS = 256


def _mod_rows(ref, rows, tm):
    n_mod, width = ref.shape
    if n_mod == 1:
        return ref[...]
    rep = tm // n_mod
    n = (rows.stop - rows.start) // rep
    a = ref[rows.start // rep:rows.start // rep + n, :]
    return jnp.broadcast_to(a[:, None, :], (n, rep, width)).reshape(n * rep, width)


def _norm_matmul_steps(x_ref, g_ref, sc_ref, sh_ref, w_ref, z_ref, h_ref):
    tm = x_ref.shape[0]

    @pl.when(pl.program_id(1) == 0)
    def _():
        g = g_ref[...]
        for r in range(tm // NORM_ROWS):
            rows = slice(r * NORM_ROWS, (r + 1) * NORM_ROWS)
            x = x_ref[rows, :]
            h = (x * _rms_scale(x) * g * (1.0 + _mod_rows(sc_ref, rows, tm))
                 + _mod_rows(sh_ref, rows, tm)).astype(BF16)
            h_ref[rows, :] = h
            z_ref[rows, :] = jnp.dot(h, w_ref[...], preferred_element_type=F32).astype(z_ref.dtype)

    @pl.when(pl.program_id(1) != 0)
    def _():
        z_ref[...] = jnp.dot(h_ref[...], w_ref[...], preferred_element_type=F32).astype(z_ref.dtype)


def _norm_matmul_cast_kernel(x_ref, g_ref, sc_ref, sh_ref, w32_ref, z_ref, wbf_ref, h_ref):
    wbf_ref[...] = w32_ref[...].astype(BF16)
    _norm_matmul_steps(x_ref, g_ref, sc_ref, sh_ref, wbf_ref, z_ref, h_ref)


def _norm_matmul_side_kernel(x_ref, g_ref, sc_ref, sh_ref, w_ref, a32_ref, b32_ref, c32_ref,
                             z_ref, abf_ref, bbf_ref, cbf_ref, h_ref, *, side_steps):
    @pl.when(pl.program_id(1) < side_steps)
    def _():
        abf_ref[...] = a32_ref[...].astype(BF16)
        bbf_ref[...] = b32_ref[...].astype(BF16)
        cbf_ref[...] = c32_ref[...].astype(BF16)

    _norm_matmul_steps(x_ref, g_ref, sc_ref, sh_ref, w_ref, z_ref, h_ref)


def _mod_spec(mod, tm, index_of_tile):
    rows = mod.shape[1]
    return pl.BlockSpec((None, rows, D_MODEL), index_of_tile)


def _norm_matmul_specs(sc, sh, tm, tn, tiles_per_mod):
    mod_idx = lambda i, j: (i // tiles_per_mod, 0, 0)
    return [
        pl.BlockSpec((tm, D_MODEL), lambda i, j: (i, 0)),
        pl.BlockSpec((1, D_MODEL), lambda i, j: (0, 0)),
        _mod_spec(sc, tm, mod_idx),
        _mod_spec(sh, tm, mod_idx),
        pl.BlockSpec((D_MODEL, tn), lambda i, j: (0, j)),
    ]


def _norm_matmul_cast(x, g, sc, sh, w32, *, tiles_per_mod):
    t = x.shape[0]
    tn = 512
    n = w32.shape[1]
    return pl.pallas_call(
        _norm_matmul_cast_kernel,
        grid=(1, n // tn),
        in_specs=_norm_matmul_specs(sc, sh, t, tn, tiles_per_mod),
        out_specs=[
            pl.BlockSpec((t, tn), lambda i, j: (i, j)),
            pl.BlockSpec((D_MODEL, tn), lambda i, j: (0, j)),
        ],
        out_shape=[
            jax.ShapeDtypeStruct((t, n), F32),
            jax.ShapeDtypeStruct(w32.shape, BF16),
        ],
        scratch_shapes=[pltpu.VMEM((t, D_MODEL), BF16)],
        compiler_params=pltpu.CompilerParams(
            dimension_semantics=("arbitrary", "arbitrary"), vmem_limit_bytes=VMEM_LIMIT_BIG),
        name="norm_in_proj_cast",
    )(x, g, sc, sh, w32)


SIDE_STEPS = 4


def _norm_matmul_side(x, g, sc, sh, w, side32, *, tm, tiles_per_mod):
    t = x.shape[0]
    tn = 1024
    n = w.shape[1]
    n_slabs = (t // tm) * SIDE_STEPS
    slab_idx = lambda i, j: (i * SIDE_STEPS + jnp.minimum(j, SIDE_STEPS - 1), 0)
    side_specs = [pl.BlockSpec((a.shape[0] // n_slabs, a.shape[1]), slab_idx) for a in side32]
    kern = functools.partial(_norm_matmul_side_kernel, side_steps=SIDE_STEPS)
    return pl.pallas_call(
        kern,
        grid=(t // tm, n // tn),
        in_specs=_norm_matmul_specs(sc, sh, tm, tn, tiles_per_mod) + side_specs,
        out_specs=[pl.BlockSpec((tm, tn), lambda i, j: (i, j))] + side_specs,
        out_shape=[jax.ShapeDtypeStruct((t, n), BF16)]
                  + [jax.ShapeDtypeStruct(a.shape, BF16) for a in side32],
        scratch_shapes=[pltpu.VMEM((tm, D_MODEL), BF16)],
        compiler_params=pltpu.CompilerParams(
            dimension_semantics=("arbitrary", "arbitrary"), vmem_limit_bytes=VMEM_LIMIT_BIG),
        name="norm_in_proj",
    )(x, g, sc, sh, w, *side32)


def _mix_prompt_kernel(z_ref, rot_ref, dmask_ref, cross_ref, sdec_ref, ws_ref, bias_ref,
                       lng_ref, lnb_ref, o_ref, sout_ref, s_ref, *, cdec):
    c = pl.program_id(1)

    @pl.when(c == 0)
    def _():
        s_ref[...] = jnp.zeros_like(s_ref)

    row = lax.broadcasted_iota(jnp.int32, (CHUNK, CHUNK), 0)
    col = lax.broadcasted_iota(jnp.int32, (CHUNK, CHUNK), 1)
    causal = row >= col
    for ci in range(z_ref.shape[0] // CHUNK):
        rows = slice(ci * CHUNK, (ci + 1) * CHUNK)
        cos_q, sin_q, cos_k, sin_k = (rot_ref[rows, i * HEAD_DIM:(i + 1) * HEAD_DIM] for i in range(4))
        for h in range(HEADS):
            lo = h * HEAD_DIM
            q = _rotary(z_ref[rows, lo:lo + HEAD_DIM].astype(F32), cos_q, sin_q)
            k = _rotary(z_ref[rows, RET_WIDTH + lo:RET_WIDTH + lo + HEAD_DIM].astype(F32), cos_k, sin_k)
            v = z_ref[rows, 2 * RET_WIDTH + lo:2 * RET_WIDTH + lo + HEAD_DIM]
            g = z_ref[rows, 3 * RET_WIDTH + lo:3 * RET_WIDTH + lo + HEAD_DIM].astype(F32)
            qb = q.astype(BF16)
            kb = k.astype(BF16)
            scores = lax.dot_general(qb, kb, (((1,), (1,)), ((), ())),
                                     preferred_element_type=F32) * dmask_ref[h]
            state = s_ref[h]
            o = (jnp.dot(scores.astype(BF16), v, preferred_element_type=F32)
                 + jnp.dot(qb, state.astype(BF16), preferred_element_type=F32) * cross_ref[h])
            kd = (k * sdec_ref[h]).astype(BF16)
            s_ref[h] = state * cdec[h] + lax.dot_general(kd, v, (((0,), (0,)), ((), ())),
                                                         preferred_element_type=F32)
            o_ref[rows, lo:lo + HEAD_DIM] = _silu_gate(g, _layernorm(o)).astype(o_ref.dtype)

        u = _gelu(z_ref[rows, 4 * RET_WIDTH:4 * RET_WIDTH + SGU_WIDTH].astype(F32))
        vn = (_layernorm(_gelu(z_ref[rows, 4 * RET_WIDTH + SGU_WIDTH:].astype(F32))) * lng_ref[...]
              + lnb_ref[...])
        for gi in range(GROUPS):
            lo = gi * CHUNK
            w = jnp.where(causal, ws_ref[gi], 0.0).astype(BF16)
            s = (jnp.dot(w, vn[:, lo:lo + CHUNK].astype(BF16), preferred_element_type=F32)
                 + bias_ref[:, lo:lo + CHUNK])
            o_ref[rows, RET_WIDTH + lo:RET_WIDTH + lo + CHUNK] = (u[:, lo:lo + CHUNK] * s).astype(o_ref.dtype)

    @pl.when(c == pl.num_programs(1) - 1)
    def _():
        sout_ref[...] = s_ref[...]


MIX_CHUNKS_PER_STEP = 4


def _mix_prompt(z, w_s, bias_tc, ln_g, ln_b, batch, seq):
    rows = MIX_CHUNKS_PER_STEP * CHUNK
    rot = np.concatenate(_rotary_tables(np.arange(seq)) + _rotary_tables(np.arange(seq), HEAD_DIM ** -0.5),
                         axis=1)
    dmask, crossb, sdecb, cdec = _decay_tables(CHUNK)
    const3 = lambda b, c: (0, 0, 0)
    const2 = lambda b, c: (0, 0)
    kern = functools.partial(_mix_prompt_kernel, cdec=cdec)
    return pl.pallas_call(
        kern,
        grid=(batch, seq // rows),
        in_specs=[
            pl.BlockSpec((None, rows, IN_WIDTH), lambda b, c: (b, c, 0)),
            pl.BlockSpec((rows, 4 * HEAD_DIM), lambda b, c: (c, 0)),
            pl.BlockSpec((HEADS, CHUNK, CHUNK), const3),
            pl.BlockSpec((HEADS, CHUNK, HEAD_DIM), const3),
            pl.BlockSpec((HEADS, CHUNK, HEAD_DIM), const3),
            pl.BlockSpec((GROUPS, CHUNK, CHUNK), const3),
            pl.BlockSpec((CHUNK, SGU_WIDTH), const2),
            pl.BlockSpec((1, SGU_WIDTH), const2),
            pl.BlockSpec((1, SGU_WIDTH), const2),
        ],
        out_specs=[
            pl.BlockSpec((None, rows, D_MODEL), lambda b, c: (b, c, 0)),
            pl.BlockSpec((None, HEADS, HEAD_DIM, HEAD_DIM), lambda b, c: (b, 0, 0, 0)),
        ],
        out_shape=[
            jax.ShapeDtypeStruct((batch, seq, D_MODEL), BF16),
            jax.ShapeDtypeStruct((batch, HEADS, HEAD_DIM, HEAD_DIM), F32),
        ],
        scratch_shapes=[pltpu.VMEM((HEADS, HEAD_DIM, HEAD_DIM), F32)],
        compiler_params=pltpu.CompilerParams(
            dimension_semantics=("arbitrary", "arbitrary"), vmem_limit_bytes=VMEM_LIMIT_SMALL),
        name="mix_prompt",
    )(z.reshape(batch, seq, IN_WIDTH), jnp.asarray(rot), jnp.asarray(dmask),
      jnp.asarray(crossb), jnp.asarray(sdecb), w_s, bias_tc, ln_g, ln_b)


def _block_diag(x, mask):
    return jnp.where(mask, jnp.concatenate([x] * HEADS, axis=0), 0.0)


def _heads_to_rows(x):
    return jnp.concatenate([x[:, h * HEAD_DIM:(h + 1) * HEAD_DIM] for h in range(HEADS)], axis=0)


def _mix_sample_kernel(z_ref, state_ref, cos_ref, sin_ref, dmask_ref, cross_ref, sdec_ref, wt_ref,
                       bias_ref, lng_ref, lnb_ref, o_ref, sout_ref, vn_ref, *, cdec, seq):
    bb = z_ref.shape[0]

    u = _gelu(z_ref[:, :, 4 * RET_WIDTH:4 * RET_WIDTH + SGU_WIDTH])
    vn = _layernorm(_gelu(z_ref[:, :, 4 * RET_WIDTH + SGU_WIDTH:])) * lng_ref[...] + lnb_ref[...]
    vn_ref[...] = vn
    trow = lax.broadcasted_iota(jnp.int32, (seq, SGU_WIDTH), 0)
    s = jnp.broadcast_to(bias_ref[...], (bb, seq, SGU_WIDTH))
    for src in range(seq):
        w = jnp.where(trow >= src, wt_ref[src], 0.0)
        s = s + w * vn[:, src:src + 1, :]
    o_ref[:, :, RET_WIDTH:] = u * s

    cos2 = cos_ref[...]
    sin2 = sin_ref[...]
    rblk = lax.broadcasted_iota(jnp.int32, (HEADS * seq, RET_WIDTH), 0) // seq
    cblk = lax.broadcasted_iota(jnp.int32, (HEADS * seq, RET_WIDTH), 1) // HEAD_DIM
    diag = rblk == cblk

    def rotary_heads(base, b):
        return jnp.concatenate(
            [_rotary(z_ref[b, :, base + h * HEAD_DIM:base + (h + 1) * HEAD_DIM], cos2, sin2)
             for h in range(HEADS)], axis=1)

    def body(b, carry):
        q = rotary_heads(0, b)
        k = rotary_heads(RET_WIDTH, b) * (HEAD_DIM ** -0.5)
        v_rows = _heads_to_rows(z_ref[b, :, 2 * RET_WIDTH:3 * RET_WIDTH]).astype(BF16)
        g_rows = _heads_to_rows(z_ref[b, :, 3 * RET_WIDTH:4 * RET_WIDTH])
        qbd = _block_diag(q, diag).astype(BF16)
        kbd = _block_diag(k, diag).astype(BF16)
        kdbd = _block_diag(k * sdec_ref[...], diag).astype(BF16)
        scores = lax.dot_general(qbd, kbd, (((1,), (1,)), ((), ())),
                                 preferred_element_type=F32) * dmask_ref[...]
        state = state_ref[b].reshape(HEADS * HEAD_DIM, HEAD_DIM)
        o = (jnp.dot(scores.astype(BF16), v_rows, preferred_element_type=F32)
             + jnp.dot(qbd, state.astype(BF16), preferred_element_type=F32) * cross_ref[...])
        out = _silu_gate(g_rows, _layernorm(o))
        ds = lax.dot_general(kdbd, v_rows, (((0,), (0,)), ((), ())), preferred_element_type=F32)
        for h in range(HEADS):
            o_ref[b, :, h * HEAD_DIM:(h + 1) * HEAD_DIM] = out[h * seq:(h + 1) * seq]
            sout_ref[b, h] = (state_ref[b, h] * cdec[h]
                              + ds[h * HEAD_DIM:(h + 1) * HEAD_DIM])
        return carry

    lax.fori_loop(0, bb, body, 0, unroll=True)


def _mix_sample(z, state, wt, bias_t, ln_g, ln_b, batch, seq):
    bb = 8
    cos2, sin2 = _rotary_tables(PAST_LEN + np.arange(seq))
    dmask, crossb, sdecb, cdec = _decay_tables(seq)
    dm = np.zeros((HEADS * seq, HEADS * seq), np.float32)
    for h in range(HEADS):
        dm[h * seq:(h + 1) * seq, h * seq:(h + 1) * seq] = dmask[h]
    cross_rows = crossb.reshape(HEADS * seq, HEAD_DIM)
    sdec_cols = np.ascontiguousarray(np.transpose(sdecb, (1, 0, 2)).reshape(seq, RET_WIDTH))
    const3 = lambda i: (0, 0, 0)
    const2 = lambda i: (0, 0)
    kern = functools.partial(_mix_sample_kernel, cdec=cdec, seq=seq)
    return pl.pallas_call(
        kern,
        grid=(batch // bb,),
        in_specs=[
            pl.BlockSpec((bb, seq, IN_WIDTH), lambda i: (i, 0, 0)),
            pl.BlockSpec((bb, HEADS, HEAD_DIM, HEAD_DIM), lambda i: (i, 0, 0, 0)),
            pl.BlockSpec((seq, HEAD_DIM), const2),
            pl.BlockSpec((seq, HEAD_DIM), const2),
            pl.BlockSpec((HEADS * seq, HEADS * seq), const2),
            pl.BlockSpec((HEADS * seq, HEAD_DIM), const2),
            pl.BlockSpec((seq, RET_WIDTH), const2),
            pl.BlockSpec((seq, seq, SGU_WIDTH), const3),
            pl.BlockSpec((seq, SGU_WIDTH), const2),
            pl.BlockSpec((1, SGU_WIDTH), const2),
            pl.BlockSpec((1, SGU_WIDTH), const2),
        ],
        out_specs=[
            pl.BlockSpec((bb, seq, D_MODEL), lambda i: (i, 0, 0)),
            pl.BlockSpec((bb, HEADS, HEAD_DIM, HEAD_DIM), lambda i: (i, 0, 0, 0)),
            pl.BlockSpec((bb, seq, SGU_WIDTH), lambda i: (i, 0, 0)),
        ],
        out_shape=[
            jax.ShapeDtypeStruct((batch, seq, D_MODEL), F32),
            jax.ShapeDtypeStruct((batch, HEADS, HEAD_DIM, HEAD_DIM), F32),
            jax.ShapeDtypeStruct((batch, seq, SGU_WIDTH), F32),
        ],
        compiler_params=pltpu.CompilerParams(
            dimension_semantics=("arbitrary",), vmem_limit_bytes=VMEM_LIMIT_SMALL),
        name="mix_sample",
    )(z.reshape(batch, seq, IN_WIDTH), state, jnp.asarray(cos2), jnp.asarray(sin2), jnp.asarray(dm),
      jnp.asarray(cross_rows), jnp.asarray(sdec_cols), wt, bias_t, ln_g, ln_b)


OUT_ROWS = 256


def _out_proj_kernel(m_ref, w_ref, x_ref, gpost_ref, gt_ref, gpre_ref, sc_ref, sh_ref,
                     x1_ref, h2_ref):
    tm = x_ref.shape[0]
    for r in range(tm // OUT_ROWS):
        rows = slice(r * OUT_ROWS, (r + 1) * OUT_ROWS)
        m = jnp.dot(m_ref[rows, :].astype(BF16), w_ref[...], preferred_element_type=F32)
        x1 = x_ref[rows, :] + _mod_rows(gt_ref, rows, tm) * (m * _rms_scale(m) * gpost_ref[...])
        x1_ref[rows, :] = x1
        h2 = (x1 * _rms_scale(x1) * gpre_ref[...] * (1.0 + _mod_rows(sc_ref, rows, tm))
              + _mod_rows(sh_ref, rows, tm))
        h2_ref[rows, :] = h2.astype(BF16)


def _out_proj(m, w, x, gpost, gt, gpre, sc, sh, *, tm, tiles_per_mod):
    t = x.shape[0]
    mod_idx = lambda i: (i // tiles_per_mod, 0, 0)
    vec = pl.BlockSpec((1, D_MODEL), lambda i: (0, 0))
    return pl.pallas_call(
        _out_proj_kernel,
        grid=(t // tm,),
        in_specs=[
            pl.BlockSpec((tm, D_MODEL), lambda i: (i, 0)),
            pl.BlockSpec((D_MODEL, D_MODEL), lambda i: (0, 0)),
            pl.BlockSpec((tm, D_MODEL), lambda i: (i, 0)),
            vec,
            _mod_spec(gt, tm, mod_idx),
            vec,
            _mod_spec(sc, tm, mod_idx),
            _mod_spec(sh, tm, mod_idx),
        ],
        out_specs=[
            pl.BlockSpec((tm, D_MODEL), lambda i: (i, 0)),
            pl.BlockSpec((tm, D_MODEL), lambda i: (i, 0)),
        ],
        out_shape=[
            jax.ShapeDtypeStruct((t, D_MODEL), F32),
            jax.ShapeDtypeStruct((t, D_MODEL), BF16),
        ],
        compiler_params=pltpu.CompilerParams(
            dimension_semantics=("arbitrary",), vmem_limit_bytes=VMEM_LIMIT_BIG),
        name="out_proj",
    )(m, w, x, gpost, gt, gpre, sc, sh)


FFN_ROWS = 512
FFN_EPI_ROWS = 256


def _ffn_kernel(h2_ref, w1_ref, w2_ref, x1_hbm, g_ref, gt_ref, y_ref, x1_buf, x1_sem):
    i = pl.program_id(0)
    k = pl.program_id(1)
    last = pl.num_programs(1) - 1
    tm = y_ref.shape[0]
    n_epi = tm // FFN_EPI_ROWS

    def x1_copy(r):
        row0 = pl.multiple_of(i * tm + r * FFN_EPI_ROWS, FFN_EPI_ROWS)
        return pltpu.make_async_copy(x1_hbm.at[pl.ds(row0, FFN_EPI_ROWS), :], x1_buf.at[r % 2],
                                     x1_sem.at[r % 2])

    @pl.when(k == 0)
    def _():
        y_ref[...] = jnp.zeros_like(y_ref)

    @pl.when(k == last)
    def _():
        x1_copy(0).start()
        x1_copy(1).start()

    for r in range(tm // FFN_ROWS):
        rows = slice(r * FFN_ROWS, (r + 1) * FFN_ROWS)
        a = jnp.dot(h2_ref[rows, :], w1_ref[...], preferred_element_type=F32)
        a = jnp.square(jnp.maximum(a, 0.0)).astype(BF16)
        y_ref[rows, :] += jnp.dot(a, w2_ref[...], preferred_element_type=F32)

    @pl.when(k == last)
    def _():
        for r in range(n_epi):
            rows = slice(r * FFN_EPI_ROWS, (r + 1) * FFN_EPI_ROWS)
            x1_copy(r).wait()
            f = y_ref[rows, :]
            y_ref[rows, :] = x1_buf[r % 2] + _mod_rows(gt_ref, rows, tm) * (f * _rms_scale(f) * g_ref[...])
            if r + 2 < n_epi:
                x1_copy(r + 2).start()


def _ffn(h2, w1, w2, x1, g, gt, *, tm, tiles_per_mod):
    t = h2.shape[0]
    tc = 1024
    assert tm % FFN_ROWS == 0 and tm // FFN_EPI_ROWS >= 2
    mod_idx = lambda i, k: (i // tiles_per_mod, 0, 0)
    return pl.pallas_call(
        _ffn_kernel,
        grid=(t // tm, D_FF // tc),
        in_specs=[
            pl.BlockSpec((tm, D_MODEL), lambda i, k: (i, 0)),
            pl.BlockSpec((D_MODEL, tc), lambda i, k: (0, k)),
            pl.BlockSpec((tc, D_MODEL), lambda i, k: (k, 0)),
            pl.BlockSpec(memory_space=pl.ANY),
            pl.BlockSpec((1, D_MODEL), lambda i, k: (0, 0)),
            _mod_spec(gt, tm, mod_idx),
        ],
        out_specs=pl.BlockSpec((tm, D_MODEL), lambda i, k: (i, 0)),
        out_shape=jax.ShapeDtypeStruct((t, D_MODEL), F32),
        scratch_shapes=[pltpu.VMEM((2, FFN_EPI_ROWS, D_MODEL), F32), pltpu.SemaphoreType.DMA((2,))],
        compiler_params=pltpu.CompilerParams(
            dimension_semantics=("arbitrary", "arbitrary"), vmem_limit_bytes=VMEM_LIMIT_BIG),
        name="ffn",
    )(h2, w1, w2, x1, g, gt)


TM_IN = 1024
TM_OUT = 512
TM_OUT_SHORT = 256
TM_FFN = 1024


def _after_in_proj(x, z, mods, state, weights, *, batch, seq, per_token_mod):
    (g_post_mix, g_pre_ffn, g_post_ffn, w_s, bias_tc, wt, bias_t, ln_g, ln_b, w_o, w_ff1, w_ff2) = weights
    gt1, sh2, sc2, gt2 = mods
    t = x.shape[0]
    tm = TM_FFN
    if per_token_mod:
        per = lambda tile: 1
    else:
        per = lambda tile: seq // tile

    def shaped(mod, tile):
        return mod.reshape(t // tile, tile // seq, D_MODEL) if per_token_mod else mod.reshape(batch, 1, D_MODEL)

    if state is None:
        mix, s_new = _mix_prompt(z, w_s, bias_tc, ln_g, ln_b, batch, seq)
        vn = None
    else:
        mix, s_new, vn = _mix_sample(z, state, wt, bias_t, ln_g, ln_b, batch, seq)
    tm_out = TM_OUT_SHORT if per_token_mod else TM_OUT
    x1, h2 = _out_proj(mix.reshape(t, D_MODEL), w_o, x, g_post_mix, shaped(gt1, tm_out), g_pre_ffn,
                       shaped(sc2, tm_out), shaped(sh2, tm_out), tm=tm_out, tiles_per_mod=per(tm_out))
    y = _ffn(h2, w_ff1, w_ff2, x1, g_post_ffn, shaped(gt2, tm), tm=tm, tiles_per_mod=per(tm))
    return y, s_new, vn


def kernel(x_prompt, x_sample, state_ret, c_prompt, c_sample, w_ada, b_ada, g_pre_mix, g_post_mix,
           g_pre_ffn, g_post_ffn, w_in, w_s, b_s, ln_g, ln_b, w_o, w_ff1, w_ff2):
    depth = w_ada.shape[0]
    batch, seq, _ = x_prompt.shape
    dec_batch, dec_seq, _ = x_sample.shape
    yp = x_prompt.reshape(batch * seq, D_MODEL)
    ys = x_sample.reshape(dec_batch * dec_seq, D_MODEL)
    c_all = jnp.concatenate([c_prompt, c_sample], axis=0)
    sp_list, ss_list, vs_list = [], [], []
    for l in range(depth):
        mod = _ada(c_all, w_ada[l], b_ada[l])
        mods_p = jnp.split(mod[:batch], 6, axis=-1)
        mods_s = jnp.split(mod[batch:], 6, axis=-1)
        bias_tc = jnp.repeat(b_s[l].T, CHUNK, axis=1)
        wt = jnp.repeat(jnp.transpose(w_s[l][:, :dec_seq, :dec_seq], (2, 1, 0)), CHUNK, axis=2)
        bias_t = bias_tc[:dec_seq]
        row = lambda v: v.reshape(1, -1)
        sh1_p, sc1_p = (m.reshape(batch, 1, D_MODEL) for m in mods_p[:2])
        sh1_s, sc1_s = (m.reshape(1, dec_batch, D_MODEL) for m in mods_s[:2])
        zs, w_in_bf = _norm_matmul_cast(ys, row(g_pre_mix[l]), sc1_s, sh1_s, w_in[l], tiles_per_mod=1)
        zp, w_o_bf, w_ff1_bf, w_ff2_bf = _norm_matmul_side(
            yp, row(g_pre_mix[l]), sc1_p, sh1_p, w_in_bf, (w_o[l], w_ff1[l], w_ff2[l]),
            tm=TM_IN, tiles_per_mod=seq // TM_IN)
        weights = (row(g_post_mix[l]), row(g_pre_ffn[l]), row(g_post_ffn[l]), w_s[l], bias_tc, wt,
                   bias_t, row(ln_g[l]), row(ln_b[l]), w_o_bf, w_ff1_bf, w_ff2_bf)
        yp, sp, _ = _after_in_proj(yp, zp, mods_p[2:], None, weights, batch=batch, seq=seq,
                                   per_token_mod=False)
        ys, ss, vn = _after_in_proj(ys, zs, mods_s[2:], state_ret[l], weights, batch=dec_batch,
                                    seq=dec_seq, per_token_mod=True)
        sp_list.append(sp)
        ss_list.append(ss)
        vs_list.append(vn)
    return (yp.reshape(batch, seq, D_MODEL), ys.reshape(dec_batch, dec_seq, D_MODEL),
            jnp.stack(sp_list), jnp.stack(ss_list), jnp.stack(vs_list))
```

```python
import functools

import numpy as np
import jax
import jax.numpy as jnp
from jax import lax
from jax.experimental import pallas as pl
from jax.experimental.pallas import tpu as pltpu

D_MODEL = 2048
RET_WIDTH = 1024
SGU_WIDTH = 1024
HEADS = 8
HEAD_DIM = 128
GROUPS = 8
CHUNK = 128
D_FF = 4 * D_MODEL
IN_WIDTH = 4 * RET_WIDTH + 2 * SGU_WIDTH
ROPE_THETA = 10000.0
EPS = 1e-6
PAST_LEN = 16384

F32 = jnp.float32
BF16 = jnp.bfloat16

VMEM_LIMIT_BIG = 58 * 1024 * 1024
VMEM_LIMIT_SMALL = 40 * 1024 * 1024


def _decay_tables(clen):
    lg = np.log(1.0 - np.power(2.0, -5.0 - np.arange(HEADS, dtype=np.float64)))
    idx = np.arange(clen, dtype=np.float64)
    diff = idx[:, None] - idx[None, :]
    dmask = np.where(diff[None] >= 0, np.exp(np.maximum(diff, 0.0)[None] * lg[:, None, None]), 0.0)
    cross = np.exp((idx + 1.0)[None, :] * lg[:, None])
    sdec = np.exp((clen - 1.0 - idx)[None, :] * lg[:, None])
    cdec = np.exp(clen * lg)
    crossb = np.broadcast_to(cross[:, :, None], (HEADS, clen, HEAD_DIM))
    sdecb = np.broadcast_to(sdec[:, :, None], (HEADS, clen, HEAD_DIM))
    return (dmask.astype(np.float32), np.ascontiguousarray(crossb, dtype=np.float32),
            np.ascontiguousarray(sdecb, dtype=np.float32), [float(c) for c in cdec])


def _rotary_tables(pos, scale=1.0):
    inv = 1.0 / (ROPE_THETA ** (np.arange(0, HEAD_DIM, 2, dtype=np.float64) / HEAD_DIM))
    ang = np.asarray(pos, dtype=np.float64)[:, None] * inv[None, :]
    cos, sin = np.cos(ang) * scale, np.sin(ang) * scale
    cos2 = np.concatenate([cos, cos], axis=-1).astype(np.float32)
    sin2 = np.concatenate([-sin, sin], axis=-1).astype(np.float32)
    return cos2, sin2


def _rms_scale(x):
    return lax.rsqrt(jnp.mean(x * x, axis=-1, keepdims=True) + EPS)


def _layernorm(x):
    mu = jnp.mean(x, axis=-1, keepdims=True)
    xc = x - mu
    var = jnp.mean(xc * xc, axis=-1, keepdims=True)
    return xc * lax.rsqrt(var + EPS)


LOG2E = 1.4426950408889634
GELU_C1 = -2.0 * np.sqrt(2.0 / np.pi) * LOG2E
GELU_C3 = GELU_C1 * 0.044715


def _gelu(x):
    return x / (1.0 + jnp.exp2(x * (GELU_C1 + GELU_C3 * (x * x))))


def _silu_gate(g, y):
    return (g * y) / (1.0 + jnp.exp2(g * (-LOG2E)))


def _rotary(x, cos2, sin2):
    return x * cos2 + pltpu.roll(x, HEAD_DIM // 2, 1) * sin2


def _ada_kernel(c_ref, w_ref, b_ref, o_ref):
    c = c_ref[...]
    s = (c * jax.nn.sigmoid(c)).astype(BF16)
    o_ref[...] = jnp.dot(s, w_ref[...].astype(BF16), preferred_element_type=F32) + b_ref[...]


def _ada(c_all, w_ada, b_ada):
    n_rows = c_all.shape[0]
    tn = 1024
    n_out = w_ada.shape[1]
    return pl.pallas_call(
        _ada_kernel,
        grid=(n_out // tn,),
        in_specs=[
            pl.BlockSpec((n_rows, D_MODEL), lambda j: (0, 0)),
            pl.BlockSpec((D_MODEL, tn), lambda j: (0, j)),
            pl.BlockSpec((1, tn), lambda j: (0, j)),
        ],
        out_specs=pl.BlockSpec((n_rows, tn), lambda j: (0, j)),
        out_shape=jax.ShapeDtypeStruct((n_rows, n_out), F32),
        compiler_params=pltpu.CompilerParams(
            dimension_semantics=("arbitrary",), vmem_limit_bytes=VMEM_LIMIT_SMALL),
        name="ada_modulation",
    )(c_all, w_ada, b_ada.reshape(1, n_out))


NORM_ROWS = 256


def _mod_rows(ref, rows, tm):
    n_mod, width = ref.shape
    if n_mod == 1:
        return ref[...]
    rep = tm // n_mod
    n = (rows.stop - rows.start) // rep
    a = ref[rows.start // rep:rows.start // rep + n, :]
    return jnp.broadcast_to(a[:, None, :], (n, rep, width)).reshape(n * rep, width)


def _norm_matmul_steps(x_ref, g_ref, sc_ref, sh_ref, w_ref, z_ref, h_ref):
    tm = x_ref.shape[0]

    @pl.when(pl.program_id(1) == 0)
    def _():
        g = g_ref[...]
        for r in range(tm // NORM_ROWS):
            rows = slice(r * NORM_ROWS, (r + 1) * NORM_ROWS)
            x = x_ref[rows, :]
            h = (x * _rms_scale(x) * g * (1.0 + _mod_rows(sc_ref, rows, tm))
                 + _mod_rows(sh_ref, rows, tm)).astype(BF16)
            h_ref[rows, :] = h
            z_ref[rows, :] = jnp.dot(h, w_ref[...], preferred_element_type=F32).astype(z_ref.dtype)

    @pl.when(pl.program_id(1) != 0)
    def _():
        z_ref[...] = jnp.dot(h_ref[...], w_ref[...], preferred_element_type=F32).astype(z_ref.dtype)


def _norm_matmul_cast_kernel(x_ref, g_ref, sc_ref, sh_ref, w32_ref, z_ref, wbf_ref, h_ref):
    wbf_ref[...] = w32_ref[...].astype(BF16)
    _norm_matmul_steps(x_ref, g_ref, sc_ref, sh_ref, wbf_ref, z_ref, h_ref)


def _norm_matmul_side_kernel(x_ref, g_ref, sc_ref, sh_ref, w_ref, a32_ref, b32_ref,
                             z_ref, abf_ref, bbf_ref, h_ref, *, side_steps):
    @pl.when(pl.program_id(1) < side_steps)
    def _():
        abf_ref[...] = a32_ref[...].astype(BF16)
        bbf_ref[...] = b32_ref[...].astype(BF16)

    _norm_matmul_steps(x_ref, g_ref, sc_ref, sh_ref, w_ref, z_ref, h_ref)


def _mod_spec(mod, tm, index_of_tile):
    rows = mod.shape[1]
    return pl.BlockSpec((None, rows, D_MODEL), index_of_tile)


def _norm_matmul_specs(sc, sh, tm, tn, tiles_per_mod):
    mod_idx = lambda i, j: (i // tiles_per_mod, 0, 0)
    return [
        pl.BlockSpec((tm, D_MODEL), lambda i, j: (i, 0)),
        pl.BlockSpec((1, D_MODEL), lambda i, j: (0, 0)),
        _mod_spec(sc, tm, mod_idx),
        _mod_spec(sh, tm, mod_idx),
        pl.BlockSpec((D_MODEL, tn), lambda i, j: (0, j)),
    ]


def _norm_matmul_cast(x, g, sc, sh, w32, *, tiles_per_mod):
    t = x.shape[0]
    tn = 512
    n = w32.shape[1]
    return pl.pallas_call(
        _norm_matmul_cast_kernel,
        grid=(1, n // tn),
        in_specs=_norm_matmul_specs(sc, sh, t, tn, tiles_per_mod),
        out_specs=[
            pl.BlockSpec((t, tn), lambda i, j: (i, j)),
            pl.BlockSpec((D_MODEL, tn), lambda i, j: (0, j)),
        ],
        out_shape=[
            jax.ShapeDtypeStruct((t, n), F32),
            jax.ShapeDtypeStruct(w32.shape, BF16),
        ],
        scratch_shapes=[pltpu.VMEM((t, D_MODEL), BF16)],
        compiler_params=pltpu.CompilerParams(
            dimension_semantics=("arbitrary", "arbitrary"), vmem_limit_bytes=VMEM_LIMIT_BIG),
        name="norm_in_proj_cast",
    )(x, g, sc, sh, w32)


SIDE_STEPS = 4


def _norm_matmul_side(x, g, sc, sh, w, side32, *, tm, tiles_per_mod):
    t = x.shape[0]
    tn = 1024
    n = w.shape[1]
    n_slabs = (t // tm) * SIDE_STEPS
    slab_idx = lambda i, j: (i * SIDE_STEPS + jnp.minimum(j, SIDE_STEPS - 1), 0)
    side_specs = [pl.BlockSpec((a.shape[0] // n_slabs, a.shape[1]), slab_idx) for a in side32]
    kern = functools.partial(_norm_matmul_side_kernel, side_steps=SIDE_STEPS)
    return pl.pallas_call(
        kern,
        grid=(t // tm, n // tn),
        in_specs=_norm_matmul_specs(sc, sh, tm, tn, tiles_per_mod) + side_specs,
        out_specs=[pl.BlockSpec((tm, tn), lambda i, j: (i, j))] + side_specs,
        out_shape=[jax.ShapeDtypeStruct((t, n), BF16)]
                  + [jax.ShapeDtypeStruct(a.shape, BF16) for a in side32],
        scratch_shapes=[pltpu.VMEM((tm, D_MODEL), BF16)],
        compiler_params=pltpu.CompilerParams(
            dimension_semantics=("arbitrary", "arbitrary"), vmem_limit_bytes=VMEM_LIMIT_BIG),
        name="norm_in_proj",
    )(x, g, sc, sh, w, *side32)


def _mix_prompt_kernel(z_ref, rot_ref, dmask_ref, cross_ref, sdec_ref, ws_ref, bias_ref,
                       lng_ref, lnb_ref, side32_ref, o_ref, sout_ref, sidebf_ref, s_ref, *, cdec):
    c = pl.program_id(1)
    sidebf_ref[...] = side32_ref[...].astype(BF16)

    @pl.when(c == 0)
    def _():
        s_ref[...] = jnp.zeros_like(s_ref)

    row = lax.broadcasted_iota(jnp.int32, (CHUNK, CHUNK), 0)
    col = lax.broadcasted_iota(jnp.int32, (CHUNK, CHUNK), 1)
    causal = row >= col
    for ci in range(z_ref.shape[0] // CHUNK):
        rows = slice(ci * CHUNK, (ci + 1) * CHUNK)
        cos_q, sin_q, cos_k, sin_k = (rot_ref[rows, i * HEAD_DIM:(i + 1) * HEAD_DIM] for i in range(4))
        for h in range(HEADS):
            lo = h * HEAD_DIM
            q = _rotary(z_ref[rows, lo:lo + HEAD_DIM].astype(F32), cos_q, sin_q)
            k = _rotary(z_ref[rows, RET_WIDTH + lo:RET_WIDTH + lo + HEAD_DIM].astype(F32), cos_k, sin_k)
            v = z_ref[rows, 2 * RET_WIDTH + lo:2 * RET_WIDTH + lo + HEAD_DIM]
            g = z_ref[rows, 3 * RET_WIDTH + lo:3 * RET_WIDTH + lo + HEAD_DIM].astype(F32)
            qb = q.astype(BF16)
            kb = k.astype(BF16)
            scores = lax.dot_general(qb, kb, (((1,), (1,)), ((), ())),
                                     preferred_element_type=F32) * dmask_ref[h]
            state = s_ref[h]
            o = (jnp.dot(scores.astype(BF16), v, preferred_element_type=F32)
                 + jnp.dot(qb, state.astype(BF16), preferred_element_type=F32) * cross_ref[h])
            kd = (k * sdec_ref[h]).astype(BF16)
            s_ref[h] = state * cdec[h] + lax.dot_general(kd, v, (((0,), (0,)), ((), ())),
                                                         preferred_element_type=F32)
            o_ref[rows, lo:lo + HEAD_DIM] = _silu_gate(g, _layernorm(o)).astype(o_ref.dtype)

        u = _gelu(z_ref[rows, 4 * RET_WIDTH:4 * RET_WIDTH + SGU_WIDTH].astype(F32))
        vn = (_layernorm(_gelu(z_ref[rows, 4 * RET_WIDTH + SGU_WIDTH:].astype(F32))) * lng_ref[...]
              + lnb_ref[...])
        for gi in range(GROUPS):
            lo = gi * CHUNK
            w = jnp.where(causal, ws_ref[gi], 0.0).astype(BF16)
            s = (jnp.dot(w, vn[:, lo:lo + CHUNK].astype(BF16), preferred_element_type=F32)
                 + bias_ref[:, lo:lo + CHUNK])
            o_ref[rows, RET_WIDTH + lo:RET_WIDTH + lo + CHUNK] = (u[:, lo:lo + CHUNK] * s).astype(o_ref.dtype)

    @pl.when(c == pl.num_programs(1) - 1)
    def _():
        sout_ref[...] = s_ref[...]


MIX_CHUNKS_PER_STEP = 4


def _mix_prompt(z, w_s, bias_tc, ln_g, ln_b, side32, batch, seq):
    rows = MIX_CHUNKS_PER_STEP * CHUNK
    steps_per_batch = seq // rows
    side_spec = pl.BlockSpec((side32.shape[0] // (batch * steps_per_batch), side32.shape[1]),
                             lambda b, c: (b * steps_per_batch + c, 0))
    rot = np.concatenate(_rotary_tables(np.arange(seq)) + _rotary_tables(np.arange(seq), HEAD_DIM ** -0.5),
                         axis=1)
    dmask, crossb, sdecb, cdec = _decay_tables(CHUNK)
    const3 = lambda b, c: (0, 0, 0)
    const2 = lambda b, c: (0, 0)
    kern = functools.partial(_mix_prompt_kernel, cdec=cdec)
    return pl.pallas_call(
        kern,
        grid=(batch, seq // rows),
        in_specs=[
            pl.BlockSpec((None, rows, IN_WIDTH), lambda b, c: (b, c, 0)),
            pl.BlockSpec((rows, 4 * HEAD_DIM), lambda b, c: (c, 0)),
            pl.BlockSpec((HEADS, CHUNK, CHUNK), const3),
            pl.BlockSpec((HEADS, CHUNK, HEAD_DIM), const3),
            pl.BlockSpec((HEADS, CHUNK, HEAD_DIM), const3),
            pl.BlockSpec((GROUPS, CHUNK, CHUNK), const3),
            pl.BlockSpec((CHUNK, SGU_WIDTH), const2),
            pl.BlockSpec((1, SGU_WIDTH), const2),
            pl.BlockSpec((1, SGU_WIDTH), const2),
            side_spec,
        ],
        out_specs=[
            pl.BlockSpec((None, rows, D_MODEL), lambda b, c: (b, c, 0)),
            pl.BlockSpec((None, HEADS, HEAD_DIM, HEAD_DIM), lambda b, c: (b, 0, 0, 0)),
            side_spec,
        ],
        out_shape=[
            jax.ShapeDtypeStruct((batch, seq, D_MODEL), BF16),
            jax.ShapeDtypeStruct((batch, HEADS, HEAD_DIM, HEAD_DIM), F32),
            jax.ShapeDtypeStruct(side32.shape, BF16),
        ],
        scratch_shapes=[pltpu.VMEM((HEADS, HEAD_DIM, HEAD_DIM), F32)],
        compiler_params=pltpu.CompilerParams(
            dimension_semantics=("arbitrary", "arbitrary"), vmem_limit_bytes=VMEM_LIMIT_SMALL),
        name="mix_prompt",
    )(z.reshape(batch, seq, IN_WIDTH), jnp.asarray(rot), jnp.asarray(dmask),
      jnp.asarray(crossb), jnp.asarray(sdecb), w_s, bias_tc, ln_g, ln_b, side32)


def _block_diag(x, mask):
    return jnp.where(mask, jnp.concatenate([x] * HEADS, axis=0), 0.0)


def _heads_to_rows(x):
    return jnp.concatenate([x[:, h * HEAD_DIM:(h + 1) * HEAD_DIM] for h in range(HEADS)], axis=0)


def _mix_sample_kernel(z_ref, state_ref, cos_ref, sin_ref, dmask_ref, cross_ref, sdec_ref, wt_ref,
                       bias_ref, lng_ref, lnb_ref, o_ref, sout_ref, vn_ref, *, cdec, seq):
    bb = z_ref.shape[0]

    u = _gelu(z_ref[:, :, 4 * RET_WIDTH:4 * RET_WIDTH + SGU_WIDTH])
    vn = _layernorm(_gelu(z_ref[:, :, 4 * RET_WIDTH + SGU_WIDTH:])) * lng_ref[...] + lnb_ref[...]
    vn_ref[...] = vn
    trow = lax.broadcasted_iota(jnp.int32, (seq, SGU_WIDTH), 0)
    s = jnp.broadcast_to(bias_ref[...], (bb, seq, SGU_WIDTH))
    for src in range(seq):
        w = jnp.where(trow >= src, wt_ref[src], 0.0)
        s = s + w * vn[:, src:src + 1, :]
    o_ref[:, :, RET_WIDTH:] = u * s

    cos2 = cos_ref[...]
    sin2 = sin_ref[...]
    rblk = lax.broadcasted_iota(jnp.int32, (HEADS * seq, RET_WIDTH), 0) // seq
    cblk = lax.broadcasted_iota(jnp.int32, (HEADS * seq, RET_WIDTH), 1) // HEAD_DIM
    diag = rblk == cblk

    def rotary_heads(base, b):
        return jnp.concatenate(
            [_rotary(z_ref[b, :, base + h * HEAD_DIM:base + (h + 1) * HEAD_DIM], cos2, sin2)
             for h in range(HEADS)], axis=1)

    def body(b, carry):
        q = rotary_heads(0, b)
        k = rotary_heads(RET_WIDTH, b) * (HEAD_DIM ** -0.5)
        v_rows = _heads_to_rows(z_ref[b, :, 2 * RET_WIDTH:3 * RET_WIDTH]).astype(BF16)
        g_rows = _heads_to_rows(z_ref[b, :, 3 * RET_WIDTH:4 * RET_WIDTH])
        qbd = _block_diag(q, diag).astype(BF16)
        kbd = _block_diag(k, diag).astype(BF16)
        kdbd = _block_diag(k * sdec_ref[...], diag).astype(BF16)
        scores = lax.dot_general(qbd, kbd, (((1,), (1,)), ((), ())),
                                 preferred_element_type=F32) * dmask_ref[...]
        state = state_ref[b].reshape(HEADS * HEAD_DIM, HEAD_DIM)
        o = (jnp.dot(scores.astype(BF16), v_rows, preferred_element_type=F32)
             + jnp.dot(qbd, state.astype(BF16), preferred_element_type=F32) * cross_ref[...])
        out = _silu_gate(g_rows, _layernorm(o))
        ds = lax.dot_general(kdbd, v_rows, (((0,), (0,)), ((), ())), preferred_element_type=F32)
        for h in range(HEADS):
            o_ref[b, :, h * HEAD_DIM:(h + 1) * HEAD_DIM] = out[h * seq:(h + 1) * seq]
            sout_ref[b, h] = (state_ref[b, h] * cdec[h]
                              + ds[h * HEAD_DIM:(h + 1) * HEAD_DIM])
        return carry

    lax.fori_loop(0, bb, body, 0, unroll=True)


def _mix_sample(z, state, wt, bias_t, ln_g, ln_b, batch, seq):
    bb = 8
    cos2, sin2 = _rotary_tables(PAST_LEN + np.arange(seq))
    dmask, crossb, sdecb, cdec = _decay_tables(seq)
    dm = np.zeros((HEADS * seq, HEADS * seq), np.float32)
    for h in range(HEADS):
        dm[h * seq:(h + 1) * seq, h * seq:(h + 1) * seq] = dmask[h]
    cross_rows = crossb.reshape(HEADS * seq, HEAD_DIM)
    sdec_cols = np.ascontiguousarray(np.transpose(sdecb, (1, 0, 2)).reshape(seq, RET_WIDTH))
    const3 = lambda i: (0, 0, 0)
    const2 = lambda i: (0, 0)
    kern = functools.partial(_mix_sample_kernel, cdec=cdec, seq=seq)
    return pl.pallas_call(
        kern,
        grid=(batch // bb,),
        in_specs=[
            pl.BlockSpec((bb, seq, IN_WIDTH), lambda i: (i, 0, 0)),
            pl.BlockSpec((bb, HEADS, HEAD_DIM, HEAD_DIM), lambda i: (i, 0, 0, 0)),
            pl.BlockSpec((seq, HEAD_DIM), const2),
            pl.BlockSpec((seq, HEAD_DIM), const2),
            pl.BlockSpec((HEADS * seq, HEADS * seq), const2),
            pl.BlockSpec((HEADS * seq, HEAD_DIM), const2),
            pl.BlockSpec((seq, RET_WIDTH), const2),
            pl.BlockSpec((seq, seq, SGU_WIDTH), const3),
            pl.BlockSpec((seq, SGU_WIDTH), const2),
            pl.BlockSpec((1, SGU_WIDTH), const2),
            pl.BlockSpec((1, SGU_WIDTH), const2),
        ],
        out_specs=[
            pl.BlockSpec((bb, seq, D_MODEL), lambda i: (i, 0, 0)),
            pl.BlockSpec((bb, HEADS, HEAD_DIM, HEAD_DIM), lambda i: (i, 0, 0, 0)),
            pl.BlockSpec((bb, seq, SGU_WIDTH), lambda i: (i, 0, 0)),
        ],
        out_shape=[
            jax.ShapeDtypeStruct((batch, seq, D_MODEL), F32),
            jax.ShapeDtypeStruct((batch, HEADS, HEAD_DIM, HEAD_DIM), F32),
            jax.ShapeDtypeStruct((batch, seq, SGU_WIDTH), F32),
        ],
        compiler_params=pltpu.CompilerParams(
            dimension_semantics=("arbitrary",), vmem_limit_bytes=VMEM_LIMIT_SMALL),
        name="mix_sample",
    )(z.reshape(batch, seq, IN_WIDTH), state, jnp.asarray(cos2), jnp.asarray(sin2), jnp.asarray(dm),
      jnp.asarray(cross_rows), jnp.asarray(sdec_cols), wt, bias_t, ln_g, ln_b)


OUT_ROWS = 256


def _out_proj_kernel(m_ref, w_ref, x_ref, gpost_ref, gt_ref, gpre_ref, sc_ref, sh_ref,
                     x1_ref, h2_ref):
    tm = x_ref.shape[0]
    for r in range(tm // OUT_ROWS):
        rows = slice(r * OUT_ROWS, (r + 1) * OUT_ROWS)
        m = jnp.dot(m_ref[rows, :].astype(BF16), w_ref[...], preferred_element_type=F32)
        x1 = x_ref[rows, :] + _mod_rows(gt_ref, rows, tm) * (m * _rms_scale(m) * gpost_ref[...])
        x1_ref[rows, :] = x1
        h2 = (x1 * _rms_scale(x1) * gpre_ref[...] * (1.0 + _mod_rows(sc_ref, rows, tm))
              + _mod_rows(sh_ref, rows, tm))
        h2_ref[rows, :] = h2.astype(BF16)


def _out_proj(m, w, x, gpost, gt, gpre, sc, sh, *, tm, tiles_per_mod):
    t = x.shape[0]
    mod_idx = lambda i: (i // tiles_per_mod, 0, 0)
    vec = pl.BlockSpec((1, D_MODEL), lambda i: (0, 0))
    return pl.pallas_call(
        _out_proj_kernel,
        grid=(t // tm,),
        in_specs=[
            pl.BlockSpec((tm, D_MODEL), lambda i: (i, 0)),
            pl.BlockSpec((D_MODEL, D_MODEL), lambda i: (0, 0)),
            pl.BlockSpec((tm, D_MODEL), lambda i: (i, 0)),
            vec,
            _mod_spec(gt, tm, mod_idx),
            vec,
            _mod_spec(sc, tm, mod_idx),
            _mod_spec(sh, tm, mod_idx),
        ],
        out_specs=[
            pl.BlockSpec((tm, D_MODEL), lambda i: (i, 0)),
            pl.BlockSpec((tm, D_MODEL), lambda i: (i, 0)),
        ],
        out_shape=[
            jax.ShapeDtypeStruct((t, D_MODEL), F32),
            jax.ShapeDtypeStruct((t, D_MODEL), BF16),
        ],
        compiler_params=pltpu.CompilerParams(
            dimension_semantics=("arbitrary",), vmem_limit_bytes=VMEM_LIMIT_BIG),
        name="out_proj",
    )(m, w, x, gpost, gt, gpre, sc, sh)


FFN_ROWS = 256
FFN_EPI_ROWS = 256


def _ffn_kernel(h2_ref, w1_ref, w2_ref, x1_hbm, g_ref, gt_ref, y_ref, x1_buf, x1_sem):
    i = pl.program_id(0)
    k = pl.program_id(1)
    last = pl.num_programs(1) - 1
    tm = y_ref.shape[0]
    n_epi = tm // FFN_EPI_ROWS

    def x1_copy(r):
        row0 = pl.multiple_of(i * tm + r * FFN_EPI_ROWS, FFN_EPI_ROWS)
        return pltpu.make_async_copy(x1_hbm.at[pl.ds(row0, FFN_EPI_ROWS), :], x1_buf.at[r], x1_sem.at[r])

    def partial_product(rows):
        a = jnp.dot(h2_ref[rows, :], w1_ref[...], preferred_element_type=F32)
        a = jnp.square(jnp.maximum(a, 0.0)).astype(BF16)
        return jnp.dot(a, w2_ref[...], preferred_element_type=F32)

    chunks = [slice(r * FFN_ROWS, (r + 1) * FFN_ROWS) for r in range(tm // FFN_ROWS)]

    @pl.when(k == last - 1)
    def _():
        for r in range(n_epi):
            x1_copy(r).start()

    @pl.when(k == 0)
    def _():
        for rows in chunks:
            y_ref[rows, :] = partial_product(rows)

    @pl.when(jnp.logical_and(k > 0, k < last))
    def _():
        for rows in chunks:
            y_ref[rows, :] += partial_product(rows)

    @pl.when(k == last)
    def _():
        for r in range(n_epi):
            x1_copy(r).wait()
        for c, rows in enumerate(chunks):
            y_ref[rows, :] += partial_product(rows)
            for r in range(c * FFN_ROWS // FFN_EPI_ROWS, (c + 1) * FFN_ROWS // FFN_EPI_ROWS):
                erows = slice(r * FFN_EPI_ROWS, (r + 1) * FFN_EPI_ROWS)
                f = y_ref[erows, :]
                y_ref[erows, :] = (x1_buf[r]
                                   + _mod_rows(gt_ref, erows, tm) * (f * _rms_scale(f) * g_ref[...]))


def _ffn(h2, w1, w2, x1, g, gt, *, tm, tiles_per_mod):
    t = h2.shape[0]
    tc = 1024
    assert tm % FFN_ROWS == 0 and FFN_ROWS % FFN_EPI_ROWS == 0 and tm // FFN_EPI_ROWS >= 2
    assert D_FF // tc >= 2
    mod_idx = lambda i, k: (i // tiles_per_mod, 0, 0)
    return pl.pallas_call(
        _ffn_kernel,
        grid=(t // tm, D_FF // tc),
        in_specs=[
            pl.BlockSpec((tm, D_MODEL), lambda i, k: (i, 0)),
            pl.BlockSpec((D_MODEL, tc), lambda i, k: (0, k)),
            pl.BlockSpec((tc, D_MODEL), lambda i, k: (k, 0)),
            pl.BlockSpec(memory_space=pl.ANY),
            pl.BlockSpec((1, D_MODEL), lambda i, k: (0, 0)),
            _mod_spec(gt, tm, mod_idx),
        ],
        out_specs=pl.BlockSpec((tm, D_MODEL), lambda i, k: (i, 0)),
        out_shape=jax.ShapeDtypeStruct((t, D_MODEL), F32),
        scratch_shapes=[pltpu.VMEM((tm // FFN_EPI_ROWS, FFN_EPI_ROWS, D_MODEL), F32),
                        pltpu.SemaphoreType.DMA((tm // FFN_EPI_ROWS,))],
        compiler_params=pltpu.CompilerParams(
            dimension_semantics=("arbitrary", "arbitrary"), vmem_limit_bytes=VMEM_LIMIT_BIG),
        name="ffn",
    )(h2, w1, w2, x1, g, gt)


TM_IN = 1024
TM_OUT = 512
TM_OUT_SHORT = 256
TM_FFN = 1024


def _out_proj_ffn(x, mix, mods, weights, *, batch, seq, per_token_mod):
    g_post_mix, g_pre_ffn, g_post_ffn, w_o, w_ff1, w_ff2 = weights
    gt1, sh2, sc2, gt2 = mods
    t = x.shape[0]
    if per_token_mod:
        per = lambda tile: 1
    else:
        per = lambda tile: seq // tile

    def shaped(mod, tile):
        return mod.reshape(t // tile, tile // seq, D_MODEL) if per_token_mod else mod.reshape(batch, 1, D_MODEL)

    tm_out = TM_OUT_SHORT if per_token_mod else TM_OUT
    x1, h2 = _out_proj(mix.reshape(t, D_MODEL), w_o, x, g_post_mix, shaped(gt1, tm_out), g_pre_ffn,
                       shaped(sc2, tm_out), shaped(sh2, tm_out), tm=tm_out, tiles_per_mod=per(tm_out))
    return _ffn(h2, w_ff1, w_ff2, x1, g_post_ffn, shaped(gt2, TM_FFN), tm=TM_FFN, tiles_per_mod=per(TM_FFN))


def kernel(x_prompt, x_sample, state_ret, c_prompt, c_sample, w_ada, b_ada, g_pre_mix, g_post_mix,
           g_pre_ffn, g_post_ffn, w_in, w_s, b_s, ln_g, ln_b, w_o, w_ff1, w_ff2):
    depth = w_ada.shape[0]
    batch, seq, _ = x_prompt.shape
    dec_batch, dec_seq, _ = x_sample.shape
    yp = x_prompt.reshape(batch * seq, D_MODEL)
    ys = x_sample.reshape(dec_batch * dec_seq, D_MODEL)
    c_all = jnp.concatenate([c_prompt, c_sample], axis=0)
    sp_list, ss_list, vs_list = [], [], []
    for l in range(depth):
        mod = _ada(c_all, w_ada[l], b_ada[l])
        mods_p = jnp.split(mod[:batch], 6, axis=-1)
        mods_s = jnp.split(mod[batch:], 6, axis=-1)
        bias_tc = jnp.repeat(b_s[l].T, CHUNK, axis=1)
        wt = jnp.repeat(jnp.transpose(w_s[l][:, :dec_seq, :dec_seq], (2, 1, 0)), CHUNK, axis=2)
        bias_t = bias_tc[:dec_seq]
        row = lambda v: v.reshape(1, -1)
        sh1_p, sc1_p = (m.reshape(batch, 1, D_MODEL) for m in mods_p[:2])
        sh1_s, sc1_s = (m.reshape(1, dec_batch, D_MODEL) for m in mods_s[:2])
        zs, w_in_bf = _norm_matmul_cast(ys, row(g_pre_mix[l]), sc1_s, sh1_s, w_in[l], tiles_per_mod=1)
        zp, w_o_bf, w_ff1_bf = _norm_matmul_side(
            yp, row(g_pre_mix[l]), sc1_p, sh1_p, w_in_bf, (w_o[l], w_ff1[l]),
            tm=TM_IN, tiles_per_mod=seq // TM_IN)
        mix_p, sp, w_ff2_bf = _mix_prompt(zp, w_s[l], bias_tc, row(ln_g[l]), row(ln_b[l]), w_ff2[l], batch, seq)
        mix_s, ss, vn = _mix_sample(zs, state_ret[l], wt, bias_t, row(ln_g[l]), row(ln_b[l]), dec_batch, dec_seq)
        weights = (row(g_post_mix[l]), row(g_pre_ffn[l]), row(g_post_ffn[l]), w_o_bf, w_ff1_bf, w_ff2_bf)
        yp = _out_proj_ffn(yp, mix_p, mods_p[2:], weights, batch=batch, seq=seq, per_token_mod=False)
        ys = _out_proj_ffn(ys, mix_s, mods_s[2:], weights, batch=dec_batch, seq=dec_seq, per_token_mod=True)
        sp_list.append(sp)
        ss_list.append(ss)
        vs_list.append(vn)
    return (yp.reshape(batch, seq, D_MODEL), ys.reshape(dec_batch, dec_seq, D_MODEL),
            jnp.stack(sp_list), jnp.stack(ss_list), jnp.stack(vs_list))
```

```python
import functools

import numpy as np
import jax
import jax.numpy as jnp
from jax import lax
from jax.experimental import pallas as pl
from jax.experimental.pallas import tpu as pltpu

D_MODEL = 2048
RET_WIDTH = 1024
SGU_WIDTH = 1024
HEADS = 8
HEAD_DIM = 128
GROUPS = 8
CHUNK = 128
D_FF = 4 * D_MODEL
IN_WIDTH = 4 * RET_WIDTH + 2 * SGU_WIDTH
ROPE_THETA = 10000.0
EPS = 1e-6
PAST_LEN = 16384

F32 = jnp.float32
BF16 = jnp.bfloat16

VMEM_LIMIT_BIG = 58 * 1024 * 1024
VMEM_LIMIT_SMALL = 40 * 1024 * 1024


def _decay_tables(clen):
    lg = np.log(1.0 - np.power(2.0, -5.0 - np.arange(HEADS, dtype=np.float64)))
    idx = np.arange(clen, dtype=np.float64)
    diff = idx[:, None] - idx[None, :]
    dmask = np.where(diff[None] >= 0, np.exp(np.maximum(diff, 0.0)[None] * lg[:, None, None]), 0.0)
    cross = np.exp((idx + 1.0)[None, :] * lg[:, None])
    sdec = np.exp((clen - 1.0 - idx)[None, :] * lg[:, None])
    cdec = np.exp(clen * lg)
    crossb = np.broadcast_to(cross[:, :, None], (HEADS, clen, HEAD_DIM))
    sdecb = np.broadcast_to(sdec[:, :, None], (HEADS, clen, HEAD_DIM))
    return (dmask.astype(np.float32), np.ascontiguousarray(crossb, dtype=np.float32),
            np.ascontiguousarray(sdecb, dtype=np.float32), [float(c) for c in cdec])


def _rotary_tables(pos, scale=1.0):
    inv = 1.0 / (ROPE_THETA ** (np.arange(0, HEAD_DIM, 2, dtype=np.float64) / HEAD_DIM))
    ang = np.asarray(pos, dtype=np.float64)[:, None] * inv[None, :]
    cos, sin = np.cos(ang) * scale, np.sin(ang) * scale
    cos2 = np.concatenate([cos, cos], axis=-1).astype(np.float32)
    sin2 = np.concatenate([-sin, sin], axis=-1).astype(np.float32)
    return cos2, sin2


def _rms_scale(x):
    return lax.rsqrt(jnp.mean(x * x, axis=-1, keepdims=True) + EPS)


def _layernorm(x):
    mu = jnp.mean(x, axis=-1, keepdims=True)
    xc = x - mu
    var = jnp.mean(xc * xc, axis=-1, keepdims=True)
    return xc * lax.rsqrt(var + EPS)


LOG2E = 1.4426950408889634
GELU_C1 = -2.0 * np.sqrt(2.0 / np.pi) * LOG2E
GELU_C3 = GELU_C1 * 0.044715


def _gelu(x):
    return x / (1.0 + jnp.exp2(x * (GELU_C1 + GELU_C3 * (x * x))))


def _silu_gate(g, y):
    return (g * y) / (1.0 + jnp.exp2(g * (-LOG2E)))


def _rotary(x, cos2, sin2):
    return x * cos2 + pltpu.roll(x, HEAD_DIM // 2, 1) * sin2


def _ada_kernel(c_ref, w_ref, b_ref, o_ref):
    c = c_ref[...]
    s = (c * jax.nn.sigmoid(c)).astype(BF16)
    o_ref[...] = jnp.dot(s, w_ref[...].astype(BF16), preferred_element_type=F32) + b_ref[...]


def _ada(c_all, w_ada, b_ada):
    n_rows = c_all.shape[0]
    tn = 2048
    n_out = w_ada.shape[1]
    return pl.pallas_call(
        _ada_kernel,
        grid=(n_out // tn,),
        in_specs=[
            pl.BlockSpec((n_rows, D_MODEL), lambda j: (0, 0)),
            pl.BlockSpec((D_MODEL, tn), lambda j: (0, j)),
            pl.BlockSpec((1, tn), lambda j: (0, j)),
        ],
        out_specs=pl.BlockSpec((n_rows, tn), lambda j: (0, j)),
        out_shape=jax.ShapeDtypeStruct((n_rows, n_out), F32),
        compiler_params=pltpu.CompilerParams(
            dimension_semantics=("arbitrary",), vmem_limit_bytes=VMEM_LIMIT_SMALL),
        name="ada_modulation",
    )(c_all, w_ada, b_ada.reshape(1, n_out))


NORM_ROWS = 256


def _mod_rows(ref, rows, tm, fn=lambda a: a):
    n_mod, width = ref.shape
    if n_mod == 1:
        return fn(ref[...])
    rep = tm // n_mod
    n = (rows.stop - rows.start) // rep
    a = fn(ref[rows.start // rep:rows.start // rep + n, :])
    return jnp.broadcast_to(a[:, None, :], (n, rep, width)).reshape(n * rep, width)


def _norm_matmul_kernel(x_ref, g_ref, sc_ref, sh_ref, w_ref, z_ref, h_ref):
    tm = x_ref.shape[0]

    @pl.when(pl.program_id(1) == 0)
    def _():
        g = g_ref[...]
        for r in range(tm // NORM_ROWS):
            rows = slice(r * NORM_ROWS, (r + 1) * NORM_ROWS)
            x = x_ref[rows, :]
            scale = _mod_rows(sc_ref, rows, tm, lambda a: g * (1.0 + a))
            h = (x * _rms_scale(x) * scale + _mod_rows(sh_ref, rows, tm)).astype(BF16)
            h_ref[rows, :] = h
            z_ref[rows, :] = jnp.dot(h, w_ref[...], preferred_element_type=F32).astype(z_ref.dtype)

    @pl.when(pl.program_id(1) != 0)
    def _():
        z_ref[...] = jnp.dot(h_ref[...], w_ref[...], preferred_element_type=F32).astype(z_ref.dtype)


def _norm_matmul_cast_kernel(x_ref, g_ref, sc_ref, sh_ref, w32_ref, z_ref, wbf_ref, h_ref):
    wbf_ref[...] = w32_ref[...].astype(BF16)
    _norm_matmul_kernel(x_ref, g_ref, sc_ref, sh_ref, wbf_ref, z_ref, h_ref)


def _mod_spec(mod, tm, index_of_tile):
    rows = mod.shape[1]
    return pl.BlockSpec((None, rows, D_MODEL), index_of_tile)


def _norm_matmul_specs(sc, sh, tm, tn, tiles_per_mod):
    mod_idx = lambda i, j: (i // tiles_per_mod, 0, 0)
    return [
        pl.BlockSpec((tm, D_MODEL), lambda i, j: (i, 0)),
        pl.BlockSpec((1, D_MODEL), lambda i, j: (0, 0)),
        _mod_spec(sc, tm, mod_idx),
        _mod_spec(sh, tm, mod_idx),
        pl.BlockSpec((D_MODEL, tn), lambda i, j: (0, j)),
    ]


def _norm_matmul_cast(x, g, sc, sh, w32, *, tiles_per_mod):
    t = x.shape[0]
    tn = 512
    n = w32.shape[1]
    return pl.pallas_call(
        _norm_matmul_cast_kernel,
        grid=(1, n // tn),
        in_specs=_norm_matmul_specs(sc, sh, t, tn, tiles_per_mod),
        out_specs=[
            pl.BlockSpec((t, tn), lambda i, j: (i, j)),
            pl.BlockSpec((D_MODEL, tn), lambda i, j: (0, j)),
        ],
        out_shape=[
            jax.ShapeDtypeStruct((t, n), F32),
            jax.ShapeDtypeStruct(w32.shape, BF16),
        ],
        scratch_shapes=[pltpu.VMEM((t, D_MODEL), BF16)],
        compiler_params=pltpu.CompilerParams(
            dimension_semantics=("arbitrary", "arbitrary"), vmem_limit_bytes=VMEM_LIMIT_BIG),
        name="norm_in_proj_cast",
    )(x, g, sc, sh, w32)


def _norm_matmul(x, g, sc, sh, w, *, tm, tiles_per_mod):
    t = x.shape[0]
    tn = 1536
    n = w.shape[1]
    return pl.pallas_call(
        _norm_matmul_kernel,
        grid=(t // tm, n // tn),
        in_specs=_norm_matmul_specs(sc, sh, tm, tn, tiles_per_mod),
        out_specs=pl.BlockSpec((tm, tn), lambda i, j: (i, j)),
        out_shape=jax.ShapeDtypeStruct((t, n), BF16),
        scratch_shapes=[pltpu.VMEM((tm, D_MODEL), BF16)],
        compiler_params=pltpu.CompilerParams(
            dimension_semantics=("arbitrary", "arbitrary"), vmem_limit_bytes=VMEM_LIMIT_BIG),
        name="norm_in_proj",
    )(x, g, sc, sh, w)


def _mix_prompt_kernel(z_ref, rot_ref, dmask_ref, cross_ref, sdec_ref, ws_ref, bias_ref,
                       lng_ref, lnb_ref, o_ref, sout_ref, s_ref, *, cdec):
    c = pl.program_id(1)

    @pl.when(c == 0)
    def _():
        s_ref[...] = jnp.zeros_like(s_ref)

    row = lax.broadcasted_iota(jnp.int32, (CHUNK, CHUNK), 0)
    col = lax.broadcasted_iota(jnp.int32, (CHUNK, CHUNK), 1)
    causal = row >= col
    for ci in range(z_ref.shape[0] // CHUNK):
        rows = slice(ci * CHUNK, (ci + 1) * CHUNK)
        cos_q, sin_q, cos_k, sin_k = (rot_ref[rows, i * HEAD_DIM:(i + 1) * HEAD_DIM] for i in range(4))
        for h in range(HEADS):
            lo = h * HEAD_DIM
            q = _rotary(z_ref[rows, lo:lo + HEAD_DIM].astype(F32), cos_q, sin_q)
            k = _rotary(z_ref[rows, RET_WIDTH + lo:RET_WIDTH + lo + HEAD_DIM].astype(F32), cos_k, sin_k)
            v = z_ref[rows, 2 * RET_WIDTH + lo:2 * RET_WIDTH + lo + HEAD_DIM]
            g = z_ref[rows, 3 * RET_WIDTH + lo:3 * RET_WIDTH + lo + HEAD_DIM].astype(F32)
            qb = q.astype(BF16)
            kb = k.astype(BF16)
            scores = lax.dot_general(qb, kb, (((1,), (1,)), ((), ())),
                                     preferred_element_type=F32) * dmask_ref[h]
            state = s_ref[h]
            o = (jnp.dot(scores.astype(BF16), v, preferred_element_type=F32)
                 + jnp.dot(qb, state.astype(BF16), preferred_element_type=F32) * cross_ref[h])
            kd = (k * sdec_ref[h]).astype(BF16)
            s_ref[h] = state * cdec[h] + lax.dot_general(kd, v, (((0,), (0,)), ((), ())),
                                                         preferred_element_type=F32)
            o_ref[rows, lo:lo + HEAD_DIM] = _silu_gate(g, _layernorm(o)).astype(o_ref.dtype)

        u = _gelu(z_ref[rows, 4 * RET_WIDTH:4 * RET_WIDTH + SGU_WIDTH].astype(F32))
        vn = (_layernorm(_gelu(z_ref[rows, 4 * RET_WIDTH + SGU_WIDTH:].astype(F32))) * lng_ref[...]
              + lnb_ref[...])
        for gi in range(GROUPS):
            lo = gi * CHUNK
            w = jnp.where(causal, ws_ref[gi], 0.0).astype(BF16)
            s = (jnp.dot(w, vn[:, lo:lo + CHUNK].astype(BF16), preferred_element_type=F32)
                 + bias_ref[:, lo:lo + CHUNK])
            o_ref[rows, RET_WIDTH + lo:RET_WIDTH + lo + CHUNK] = (u[:, lo:lo + CHUNK] * s).astype(o_ref.dtype)

    @pl.when(c == pl.num_programs(1) - 1)
    def _():
        sout_ref[...] = s_ref[...]


MIX_CHUNKS_PER_STEP = 4


def _mix_prompt(z, w_s, bias_tc, ln_g, ln_b, batch, seq):
    rows = MIX_CHUNKS_PER_STEP * CHUNK
    rot = np.concatenate(_rotary_tables(np.arange(seq)) + _rotary_tables(np.arange(seq), HEAD_DIM ** -0.5),
                         axis=1)
    dmask, crossb, sdecb, cdec = _decay_tables(CHUNK)
    const3 = lambda b, c: (0, 0, 0)
    const2 = lambda b, c: (0, 0)
    kern = functools.partial(_mix_prompt_kernel, cdec=cdec)
    return pl.pallas_call(
        kern,
        grid=(batch, seq // rows),
        in_specs=[
            pl.BlockSpec((None, rows, IN_WIDTH), lambda b, c: (b, c, 0)),
            pl.BlockSpec((rows, 4 * HEAD_DIM), lambda b, c: (c, 0)),
            pl.BlockSpec((HEADS, CHUNK, CHUNK), const3),
            pl.BlockSpec((HEADS, CHUNK, HEAD_DIM), const3),
            pl.BlockSpec((HEADS, CHUNK, HEAD_DIM), const3),
            pl.BlockSpec((GROUPS, CHUNK, CHUNK), const3),
            pl.BlockSpec((CHUNK, SGU_WIDTH), const2),
            pl.BlockSpec((1, SGU_WIDTH), const2),
            pl.BlockSpec((1, SGU_WIDTH), const2),
        ],
        out_specs=[
            pl.BlockSpec((None, rows, D_MODEL), lambda b, c: (b, c, 0)),
            pl.BlockSpec((None, HEADS, HEAD_DIM, HEAD_DIM), lambda b, c: (b, 0, 0, 0)),
        ],
        out_shape=[
            jax.ShapeDtypeStruct((batch, seq, D_MODEL), BF16),
            jax.ShapeDtypeStruct((batch, HEADS, HEAD_DIM, HEAD_DIM), F32),
        ],
        scratch_shapes=[pltpu.VMEM((HEADS, HEAD_DIM, HEAD_DIM), F32)],
        compiler_params=pltpu.CompilerParams(
            dimension_semantics=("arbitrary", "arbitrary"), vmem_limit_bytes=VMEM_LIMIT_SMALL),
        name="mix_prompt",
    )(z.reshape(batch, seq, IN_WIDTH), jnp.asarray(rot), jnp.asarray(dmask),
      jnp.asarray(crossb), jnp.asarray(sdecb), w_s, bias_tc, ln_g, ln_b)


def _block_diag(x, mask):
    return jnp.where(mask, jnp.concatenate([x] * HEADS, axis=0), 0.0)


def _heads_to_rows(x):
    return jnp.concatenate([x[:, h * HEAD_DIM:(h + 1) * HEAD_DIM] for h in range(HEADS)], axis=0)


def _mix_sample_kernel(z_ref, state_ref, cos_ref, sin_ref, dmask_ref, cross_ref, sdec_ref, wt_ref,
                       bias_ref, lng_ref, lnb_ref, o_ref, sout_ref, vn_ref, *, cdec, seq):
    bb = z_ref.shape[0]

    u = _gelu(z_ref[:, :, 4 * RET_WIDTH:4 * RET_WIDTH + SGU_WIDTH])
    vn = _layernorm(_gelu(z_ref[:, :, 4 * RET_WIDTH + SGU_WIDTH:])) * lng_ref[...] + lnb_ref[...]
    vn_ref[...] = vn
    trow = lax.broadcasted_iota(jnp.int32, (seq, SGU_WIDTH), 0)
    s = jnp.broadcast_to(bias_ref[...], (bb, seq, SGU_WIDTH))
    for src in range(seq):
        w = jnp.where(trow >= src, wt_ref[src], 0.0)
        s = s + w * vn[:, src:src + 1, :]
    o_ref[:, :, RET_WIDTH:] = u * s

    cos2 = cos_ref[...]
    sin2 = sin_ref[...]
    rblk = lax.broadcasted_iota(jnp.int32, (HEADS * seq, RET_WIDTH), 0) // seq
    cblk = lax.broadcasted_iota(jnp.int32, (HEADS * seq, RET_WIDTH), 1) // HEAD_DIM
    diag = rblk == cblk

    def rotary_heads(base, b):
        return jnp.concatenate(
            [_rotary(z_ref[b, :, base + h * HEAD_DIM:base + (h + 1) * HEAD_DIM], cos2, sin2)
             for h in range(HEADS)], axis=1)

    def body(b, carry):
        q = rotary_heads(0, b)
        k = rotary_heads(RET_WIDTH, b) * (HEAD_DIM ** -0.5)
        v_rows = _heads_to_rows(z_ref[b, :, 2 * RET_WIDTH:3 * RET_WIDTH]).astype(BF16)
        g_rows = _heads_to_rows(z_ref[b, :, 3 * RET_WIDTH:4 * RET_WIDTH])
        qbd = _block_diag(q, diag).astype(BF16)
        kbd = _block_diag(k, diag).astype(BF16)
        kdbd = _block_diag(k * sdec_ref[...], diag).astype(BF16)
        scores = lax.dot_general(qbd, kbd, (((1,), (1,)), ((), ())),
                                 preferred_element_type=F32) * dmask_ref[...]
        state = state_ref[b].reshape(HEADS * HEAD_DIM, HEAD_DIM)
        o = (jnp.dot(scores.astype(BF16), v_rows, preferred_element_type=F32)
             + jnp.dot(qbd, state.astype(BF16), preferred_element_type=F32) * cross_ref[...])
        out = _silu_gate(g_rows, _layernorm(o))
        ds = lax.dot_general(kdbd, v_rows, (((0,), (0,)), ((), ())), preferred_element_type=F32)
        for h in range(HEADS):
            o_ref[b, :, h * HEAD_DIM:(h + 1) * HEAD_DIM] = out[h * seq:(h + 1) * seq]
            sout_ref[b, h] = (state_ref[b, h] * cdec[h]
                              + ds[h * HEAD_DIM:(h + 1) * HEAD_DIM])
        return carry

    lax.fori_loop(0, bb, body, 0, unroll=True)


def _mix_sample(z, state, wt, bias_t, ln_g, ln_b, batch, seq):
    bb = 16
    cos2, sin2 = _rotary_tables(PAST_LEN + np.arange(seq))
    dmask, crossb, sdecb, cdec = _decay_tables(seq)
    dm = np.zeros((HEADS * seq, HEADS * seq), np.float32)
    for h in range(HEADS):
        dm[h * seq:(h + 1) * seq, h * seq:(h + 1) * seq] = dmask[h]
    cross_rows = crossb.reshape(HEADS * seq, HEAD_DIM)
    sdec_cols = np.ascontiguousarray(np.transpose(sdecb, (1, 0, 2)).reshape(seq, RET_WIDTH))
    const3 = lambda i: (0, 0, 0)
    const2 = lambda i: (0, 0)
    kern = functools.partial(_mix_sample_kernel, cdec=cdec, seq=seq)
    return pl.pallas_call(
        kern,
        grid=(batch // bb,),
        in_specs=[
            pl.BlockSpec((bb, seq, IN_WIDTH), lambda i: (i, 0, 0)),
            pl.BlockSpec((bb, HEADS, HEAD_DIM, HEAD_DIM), lambda i: (i, 0, 0, 0)),
            pl.BlockSpec((seq, HEAD_DIM), const2),
            pl.BlockSpec((seq, HEAD_DIM), const2),
            pl.BlockSpec((HEADS * seq, HEADS * seq), const2),
            pl.BlockSpec((HEADS * seq, HEAD_DIM), const2),
            pl.BlockSpec((seq, RET_WIDTH), const2),
            pl.BlockSpec((seq, seq, SGU_WIDTH), const3),
            pl.BlockSpec((seq, SGU_WIDTH), const2),
            pl.BlockSpec((1, SGU_WIDTH), const2),
            pl.BlockSpec((1, SGU_WIDTH), const2),
        ],
        out_specs=[
            pl.BlockSpec((bb, seq, D_MODEL), lambda i: (i, 0, 0)),
            pl.BlockSpec((bb, HEADS, HEAD_DIM, HEAD_DIM), lambda i: (i, 0, 0, 0)),
            pl.BlockSpec((bb, seq, SGU_WIDTH), lambda i: (i, 0, 0)),
        ],
        out_shape=[
            jax.ShapeDtypeStruct((batch, seq, D_MODEL), F32),
            jax.ShapeDtypeStruct((batch, HEADS, HEAD_DIM, HEAD_DIM), F32),
            jax.ShapeDtypeStruct((batch, seq, SGU_WIDTH), F32),
        ],
        compiler_params=pltpu.CompilerParams(
            dimension_semantics=("arbitrary",), vmem_limit_bytes=VMEM_LIMIT_BIG),
        name="mix_sample",
    )(z.reshape(batch, seq, IN_WIDTH), state, jnp.asarray(cos2), jnp.asarray(sin2), jnp.asarray(dm),
      jnp.asarray(cross_rows), jnp.asarray(sdec_cols), wt, bias_t, ln_g, ln_b)


OUT_TAIL_ROWS = 128


def _out_proj_kernel(m_ref, w_ref, x_ref, gpost_ref, gt_ref, gpre_ref, sc_ref, sh_ref,
                     x1_ref, h2_ref, *wbf_out):
    tm = x_ref.shape[0]
    if wbf_out:
        w_ref, w32_ref = wbf_out[0], w_ref

        @pl.when(pl.program_id(0) == 0)
        def _():
            w_ref[...] = w32_ref[...].astype(BF16)
    bounds = [0, tm - OUT_TAIL_ROWS, tm] if tm > OUT_TAIL_ROWS else [0, tm]
    for lo, hi in zip(bounds[:-1], bounds[1:]):
        rows = slice(lo, hi)
        m = jnp.dot(m_ref[rows, :].astype(BF16), w_ref[...], preferred_element_type=F32)
        gate = _mod_rows(gt_ref, rows, tm, lambda a: a * gpost_ref[...])
        x1 = x_ref[rows, :] + m * _rms_scale(m) * gate
        x1_ref[rows, :] = x1
        scale = _mod_rows(sc_ref, rows, tm, lambda a: gpre_ref[...] * (1.0 + a))
        h2 = x1 * _rms_scale(x1) * scale + _mod_rows(sh_ref, rows, tm)
        h2_ref[rows, :] = h2.astype(BF16)


def _out_proj(m, w, x, gpost, gt, gpre, sc, sh, *, tm, tiles_per_mod):
    t = x.shape[0]
    mod_idx = lambda i: (i // tiles_per_mod, 0, 0)
    vec = pl.BlockSpec((1, D_MODEL), lambda i: (0, 0))
    w_spec = pl.BlockSpec((D_MODEL, D_MODEL), lambda i: (0, 0))
    cast_w = w.dtype != BF16
    return pl.pallas_call(
        _out_proj_kernel,
        grid=(t // tm,),
        in_specs=[
            pl.BlockSpec((tm, D_MODEL), lambda i: (i, 0)),
            w_spec,
            pl.BlockSpec((tm, D_MODEL), lambda i: (i, 0)),
            vec,
            _mod_spec(gt, tm, mod_idx),
            vec,
            _mod_spec(sc, tm, mod_idx),
            _mod_spec(sh, tm, mod_idx),
        ],
        out_specs=[
            pl.BlockSpec((tm, D_MODEL), lambda i: (i, 0)),
            pl.BlockSpec((tm, D_MODEL), lambda i: (i, 0)),
        ] + [w_spec] * cast_w,
        out_shape=[
            jax.ShapeDtypeStruct((t, D_MODEL), F32),
            jax.ShapeDtypeStruct((t, D_MODEL), BF16),
        ] + [jax.ShapeDtypeStruct(w.shape, BF16)] * cast_w,
        compiler_params=pltpu.CompilerParams(
            dimension_semantics=("arbitrary",), vmem_limit_bytes=VMEM_LIMIT_BIG),
        name="out_proj_cast" if cast_w else "out_proj",
    )(m, w, x, gpost, gt, gpre, sc, sh)


FFN_ROWS = 256
FFN_EPI_ROWS = 256


def _ffn_kernel(h2_ref, w1_ref, w2_ref, x1_hbm, g_ref, gt_ref, y_ref, *rest):
    x1_buf, x1_sem = rest[-2:]
    if len(rest) == 4:
        w1_ref, w2_ref, w1_32, w2_32 = rest[0], rest[1], w1_ref, w2_ref
        w1_ref[...] = w1_32[...].astype(BF16)
        w2_ref[...] = w2_32[...].astype(BF16)
    i = pl.program_id(0)
    k = pl.program_id(1)
    last = pl.num_programs(1) - 1
    tm = y_ref.shape[0]
    n_epi = tm // FFN_EPI_ROWS

    def x1_copy(r):
        row0 = pl.multiple_of(i * tm + r * FFN_EPI_ROWS, FFN_EPI_ROWS)
        return pltpu.make_async_copy(x1_hbm.at[pl.ds(row0, FFN_EPI_ROWS), :], x1_buf.at[r], x1_sem.at[r])

    def partial_product(rows):
        a = jnp.dot(h2_ref[rows, :], w1_ref[...], preferred_element_type=F32)
        a = jnp.square(jnp.maximum(a, 0.0)).astype(BF16)
        return jnp.dot(a, w2_ref[...], preferred_element_type=F32)

    chunks = [slice(r * FFN_ROWS, (r + 1) * FFN_ROWS) for r in range(tm // FFN_ROWS)]

    @pl.when(k == last - 1)
    def _():
        for r in range(n_epi):
            x1_copy(r).start()

    @pl.when(k == 0)
    def _():
        for rows in chunks:
            y_ref[rows, :] = partial_product(rows)

    @pl.when(jnp.logical_and(k > 0, k < last))
    def _():
        for rows in chunks:
            y_ref[rows, :] += partial_product(rows)

    @pl.when(k == last)
    def _():
        for r in range(n_epi):
            x1_copy(r).wait()
        for c, rows in enumerate(chunks):
            y_ref[rows, :] += partial_product(rows)
            for r in range(c * FFN_ROWS // FFN_EPI_ROWS, (c + 1) * FFN_ROWS // FFN_EPI_ROWS):
                erows = slice(r * FFN_EPI_ROWS, (r + 1) * FFN_EPI_ROWS)
                f = y_ref[erows, :]
                gate = _mod_rows(gt_ref, erows, tm, lambda a: a * g_ref[...])
                y_ref[erows, :] = x1_buf[r] + f * _rms_scale(f) * gate


def _ffn(h2, w1, w2, x1, g, gt, *, tm, tiles_per_mod):
    t = h2.shape[0]
    cast_w = w1.dtype != BF16
    tc = 512 if cast_w else 1024
    assert not cast_w or t == tm
    w1_spec = pl.BlockSpec((D_MODEL, tc), lambda i, k: (0, k))
    w2_spec = pl.BlockSpec((tc, D_MODEL), lambda i, k: (k, 0))
    assert tm % FFN_ROWS == 0 and FFN_ROWS % FFN_EPI_ROWS == 0 and tm // FFN_EPI_ROWS >= 2
    assert D_FF // tc >= 2
    mod_idx = lambda i, k: (i // tiles_per_mod, 0, 0)
    return pl.pallas_call(
        _ffn_kernel,
        grid=(t // tm, D_FF // tc),
        in_specs=[
            pl.BlockSpec((tm, D_MODEL), lambda i, k: (i, 0)),
            w1_spec,
            w2_spec,
            pl.BlockSpec(memory_space=pl.ANY),
            pl.BlockSpec((1, D_MODEL), lambda i, k: (0, 0)),
            _mod_spec(gt, tm, mod_idx),
        ],
        out_specs=[pl.BlockSpec((tm, D_MODEL), lambda i, k: (i, 0))] + [w1_spec, w2_spec] * cast_w,
        out_shape=[jax.ShapeDtypeStruct((t, D_MODEL), F32)]
                  + [jax.ShapeDtypeStruct(w1.shape, BF16), jax.ShapeDtypeStruct(w2.shape, BF16)] * cast_w,
        scratch_shapes=[pltpu.VMEM((tm // FFN_EPI_ROWS, FFN_EPI_ROWS, D_MODEL), F32),
                        pltpu.SemaphoreType.DMA((tm // FFN_EPI_ROWS,))],
        compiler_params=pltpu.CompilerParams(
            dimension_semantics=("arbitrary", "arbitrary"), vmem_limit_bytes=VMEM_LIMIT_BIG),
        name="ffn_cast" if cast_w else "ffn",
    )(h2, w1, w2, x1, g, gt)


TM_IN = 1024
TM_OUT = 512
TM_OUT_SHORT = 256
TM_FFN = 1024


def _out_proj_ffn(x, mix, mods, weights, *, batch, seq, per_token_mod):
    g_post_mix, g_pre_ffn, g_post_ffn, w_o, w_ff1, w_ff2 = weights
    gt1, sh2, sc2, gt2 = mods
    t = x.shape[0]
    if per_token_mod:
        per = lambda tile: 1
    else:
        per = lambda tile: seq // tile

    def shaped(mod, tile):
        return mod.reshape(t // tile, tile // seq, D_MODEL) if per_token_mod else mod.reshape(batch, 1, D_MODEL)

    tm_out = TM_OUT_SHORT if per_token_mod else TM_OUT
    x1, h2, *w_o_bf = _out_proj(mix.reshape(t, D_MODEL), w_o, x, g_post_mix, shaped(gt1, tm_out), g_pre_ffn,
                                shaped(sc2, tm_out), shaped(sh2, tm_out), tm=tm_out,
                                tiles_per_mod=per(tm_out))
    y, *w_ff_bf = _ffn(h2, w_ff1, w_ff2, x1, g_post_ffn, shaped(gt2, TM_FFN), tm=TM_FFN,
                       tiles_per_mod=per(TM_FFN))
    return y, tuple(w_o_bf or [w_o]) + tuple(w_ff_bf or [w_ff1, w_ff2])


def kernel(x_prompt, x_sample, state_ret, c_prompt, c_sample, w_ada, b_ada, g_pre_mix, g_post_mix,
           g_pre_ffn, g_post_ffn, w_in, w_s, b_s, ln_g, ln_b, w_o, w_ff1, w_ff2):
    depth = w_ada.shape[0]
    batch, seq, _ = x_prompt.shape
    dec_batch, dec_seq, _ = x_sample.shape
    yp = x_prompt.reshape(batch * seq, D_MODEL)
    ys = x_sample.reshape(dec_batch * dec_seq, D_MODEL)
    c_all = jnp.concatenate([c_prompt, c_sample], axis=0)
    sp_list, ss_list, vs_list = [], [], []
    for l in range(depth):
        mod = _ada(c_all, w_ada[l], b_ada[l])
        mods_p = jnp.split(mod[:batch], 6, axis=-1)
        mods_s = jnp.split(mod[batch:], 6, axis=-1)
        bias_tc = jnp.repeat(b_s[l].T, CHUNK, axis=1)
        wt = jnp.repeat(jnp.transpose(w_s[l][:, :dec_seq, :dec_seq], (2, 1, 0)), CHUNK, axis=2)
        bias_t = bias_tc[:dec_seq]
        row = lambda v: v.reshape(1, -1)
        sh1_p, sc1_p = (m.reshape(batch, 1, D_MODEL) for m in mods_p[:2])
        sh1_s, sc1_s = (m.reshape(1, dec_batch, D_MODEL) for m in mods_s[:2])
        gains = (row(g_post_mix[l]), row(g_pre_ffn[l]), row(g_post_ffn[l]))
        zs, w_in_bf = _norm_matmul_cast(ys, row(g_pre_mix[l]), sc1_s, sh1_s, w_in[l], tiles_per_mod=1)
        mix_s, ss, vn = _mix_sample(zs, state_ret[l], wt, bias_t, row(ln_g[l]), row(ln_b[l]), dec_batch, dec_seq)
        ys, weights_bf = _out_proj_ffn(ys, mix_s, mods_s[2:], gains + (w_o[l], w_ff1[l], w_ff2[l]),
                                       batch=dec_batch, seq=dec_seq, per_token_mod=True)
        zp = _norm_matmul(yp, row(g_pre_mix[l]), sc1_p, sh1_p, w_in_bf, tm=TM_IN, tiles_per_mod=seq // TM_IN)
        mix_p, sp = _mix_prompt(zp, w_s[l], bias_tc, row(ln_g[l]), row(ln_b[l]), batch, seq)
        yp, _ = _out_proj_ffn(yp, mix_p, mods_p[2:], gains + weights_bf, batch=batch, seq=seq,
                              per_token_mod=False)
        sp_list.append(sp)
        ss_list.append(ss)
        vs_list.append(vn)
    return (yp.reshape(batch, seq, D_MODEL), ys.reshape(dec_batch, dec_seq, D_MODEL),
            jnp.stack(sp_list), jnp.stack(ss_list), jnp.stack(vs_list))
```

```python
import functools

import numpy as np
import jax
import jax.numpy as jnp
from jax import lax
from jax.experimental import pallas as pl
from jax.experimental.pallas import tpu as pltpu

D_MODEL = 2048
RET_WIDTH = 1024
SGU_WIDTH = 1024
HEADS = 8
HEAD_DIM = 128
GROUPS = 8
CHUNK = 128
D_FF = 4 * D_MODEL
IN_WIDTH = 4 * RET_WIDTH + 2 * SGU_WIDTH
ROPE_THETA = 10000.0
EPS = 1e-6
PAST_LEN = 16384

F32 = jnp.float32
BF16 = jnp.bfloat16

VMEM_LIMIT_BIG = 58 * 1024 * 1024
VMEM_LIMIT_SMALL = 40 * 1024 * 1024


def _decay_tables(clen):
    lg = np.log(1.0 - np.power(2.0, -5.0 - np.arange(HEADS, dtype=np.float64)))
    idx = np.arange(clen, dtype=np.float64)
    diff = idx[:, None] - idx[None, :]
    dmask = np.where(diff[None] >= 0, np.exp(np.maximum(diff, 0.0)[None] * lg[:, None, None]), 0.0)
    cross = np.exp((idx + 1.0)[None, :] * lg[:, None])
    sdec = np.exp((clen - 1.0 - idx)[None, :] * lg[:, None])
    cdec = np.exp(clen * lg)
    crossb = np.broadcast_to(cross[:, :, None], (HEADS, clen, HEAD_DIM))
    sdecb = np.broadcast_to(sdec[:, :, None], (HEADS, clen, HEAD_DIM))
    return (dmask.astype(np.float32), np.ascontiguousarray(crossb, dtype=np.float32),
            np.ascontiguousarray(sdecb, dtype=np.float32), [float(c) for c in cdec])


def _rotary_tables(pos, scale=1.0):
    inv = 1.0 / (ROPE_THETA ** (np.arange(0, HEAD_DIM, 2, dtype=np.float64) / HEAD_DIM))
    ang = np.asarray(pos, dtype=np.float64)[:, None] * inv[None, :]
    cos, sin = np.cos(ang) * scale, np.sin(ang) * scale
    cos2 = np.concatenate([cos, cos], axis=-1).astype(np.float32)
    sin2 = np.concatenate([-sin, sin], axis=-1).astype(np.float32)
    return cos2, sin2


def _rms_scale(x):
    return lax.rsqrt(jnp.mean(x * x, axis=-1, keepdims=True) + EPS)


def _layernorm(x):
    mu = jnp.mean(x, axis=-1, keepdims=True)
    xc = x - mu
    var = jnp.mean(xc * xc, axis=-1, keepdims=True)
    return xc * lax.rsqrt(var + EPS)


LOG2E = 1.4426950408889634
GELU_C1 = -2.0 * np.sqrt(2.0 / np.pi) * LOG2E
GELU_C3 = GELU_C1 * 0.044715


def _gelu(x):
    return x / (1.0 + jnp.exp2(x * (GELU_C1 + GELU_C3 * (x * x))))


def _silu_gate(g, y):
    return (g * y) / (1.0 + jnp.exp2(g * (-LOG2E)))


def _rotary(x, cos2, sin2):
    return x * cos2 + pltpu.roll(x, HEAD_DIM // 2, 1) * sin2


def _ada_kernel(ca_ref, cb_ref, w_ref, b_ref, oa_ref, ob_ref):
    c = jnp.concatenate([ca_ref[...], cb_ref[...]], axis=0)
    s = (c * jax.nn.sigmoid(c)).astype(BF16)
    r = jnp.dot(s, w_ref[...].astype(BF16), preferred_element_type=F32) + b_ref[...]
    oa_ref[...] = r[:ca_ref.shape[0]]
    ob_ref[...] = r[ca_ref.shape[0]:]


def _ada(c_a, c_b, w_ada, b_ada):
    tn = 1024
    n_out = w_ada.shape[1]
    rows = lambda c: pl.BlockSpec((c.shape[0], D_MODEL), lambda j: (0, 0))
    cols = lambda c: pl.BlockSpec((c.shape[0], tn), lambda j: (0, j))
    return pl.pallas_call(
        _ada_kernel,
        grid=(n_out // tn,),
        in_specs=[
            rows(c_a),
            rows(c_b),
            pl.BlockSpec((D_MODEL, tn), lambda j: (0, j)),
            pl.BlockSpec((1, tn), lambda j: (0, j)),
        ],
        out_specs=[cols(c_a), cols(c_b)],
        out_shape=[jax.ShapeDtypeStruct((c_a.shape[0], n_out), F32),
                   jax.ShapeDtypeStruct((c_b.shape[0], n_out), F32)],
        compiler_params=pltpu.CompilerParams(
            dimension_semantics=("arbitrary",), vmem_limit_bytes=VMEM_LIMIT_SMALL),
        name="ada_modulation",
    )(c_a, c_b, w_ada, b_ada.reshape(1, n_out))


NORM_ROWS = 256


def _mod_rows(ref, rows, tm, fn=lambda a: a):
    n_mod, width = ref.shape
    if n_mod == 1:
        return fn(ref[...])
    rep = tm // n_mod
    n = (rows.stop - rows.start) // rep
    a = fn(ref[rows.start // rep:rows.start // rep + n, :])
    return jnp.broadcast_to(a[:, None, :], (n, rep, width)).reshape(n * rep, width)


def _norm_matmul_kernel(x_ref, g_ref, sc_ref, sh_ref, w_ref, z_ref, h_ref):
    tm = x_ref.shape[0]

    @pl.when(pl.program_id(1) == 0)
    def _():
        g = g_ref[...]
        for r in range(tm // NORM_ROWS):
            rows = slice(r * NORM_ROWS, (r + 1) * NORM_ROWS)
            x = x_ref[rows, :]
            scale = _mod_rows(sc_ref, rows, tm, lambda a: g * (1.0 + a))
            h = (x * _rms_scale(x) * scale + _mod_rows(sh_ref, rows, tm)).astype(BF16)
            h_ref[rows, :] = h
            z_ref[rows, :] = jnp.dot(h, w_ref[...], preferred_element_type=F32).astype(z_ref.dtype)

    @pl.when(pl.program_id(1) != 0)
    def _():
        z_ref[...] = jnp.dot(h_ref[...], w_ref[...], preferred_element_type=F32).astype(z_ref.dtype)


def _norm_matmul_cast_kernel(x_ref, g_ref, sc_ref, sh_ref, w32_ref, z_ref, wbf_ref, h_ref):
    wbf_ref[...] = w32_ref[...].astype(BF16)
    _norm_matmul_kernel(x_ref, g_ref, sc_ref, sh_ref, wbf_ref, z_ref, h_ref)


def _mod_spec(mod, tm, index_of_tile):
    array, col = mod
    return pl.BlockSpec((None, array.shape[1], D_MODEL), lambda *ids: (index_of_tile(*ids)[0], 0, col))


def _norm_matmul_specs(sc, sh, tm, tn, tiles_per_mod):
    mod_idx = lambda i, j: (i // tiles_per_mod, 0, 0)
    return [
        pl.BlockSpec((tm, D_MODEL), lambda i, j: (i, 0)),
        pl.BlockSpec((1, D_MODEL), lambda i, j: (0, 0)),
        _mod_spec(sc, tm, mod_idx),
        _mod_spec(sh, tm, mod_idx),
        pl.BlockSpec((D_MODEL, tn), lambda i, j: (0, j)),
    ]


def _norm_matmul_cast(x, g, sc, sh, w32, *, tiles_per_mod):
    t = x.shape[0]
    tn = 1024
    n = w32.shape[1]
    return pl.pallas_call(
        _norm_matmul_cast_kernel,
        grid=(1, n // tn),
        in_specs=_norm_matmul_specs(sc, sh, t, tn, tiles_per_mod),
        out_specs=[
            pl.BlockSpec((t, tn), lambda i, j: (i, j)),
            pl.BlockSpec((D_MODEL, tn), lambda i, j: (0, j)),
        ],
        out_shape=[
            jax.ShapeDtypeStruct((t, n), F32),
            jax.ShapeDtypeStruct(w32.shape, BF16),
        ],
        scratch_shapes=[pltpu.VMEM((t, D_MODEL), BF16)],
        compiler_params=pltpu.CompilerParams(
            dimension_semantics=("arbitrary", "arbitrary"), vmem_limit_bytes=VMEM_LIMIT_BIG),
        name="norm_in_proj_cast",
    )(x, g, sc[0], sh[0], w32)


def _norm_matmul(x, g, sc, sh, w, *, tm, tiles_per_mod):
    t = x.shape[0]
    tn = 2048
    n = w.shape[1]
    return pl.pallas_call(
        _norm_matmul_kernel,
        grid=(t // tm, n // tn),
        in_specs=_norm_matmul_specs(sc, sh, tm, tn, tiles_per_mod),
        out_specs=pl.BlockSpec((tm, tn), lambda i, j: (i, j)),
        out_shape=jax.ShapeDtypeStruct((t, n), BF16),
        scratch_shapes=[pltpu.VMEM((tm, D_MODEL), BF16)],
        compiler_params=pltpu.CompilerParams(
            dimension_semantics=("arbitrary", "arbitrary"), vmem_limit_bytes=VMEM_LIMIT_BIG),
        name="norm_in_proj",
    )(x, g, sc[0], sh[0], w)


def _mix_prompt_kernel(z_ref, rot_ref, dmask_ref, cross_ref, sdec_ref, ws_ref, bias_ref,
                       lng_ref, lnb_ref, side32_ref, o_ref, sout_ref, sidebf_ref, s_ref, *, cdec):
    c = pl.program_id(1)
    sidebf_ref[...] = side32_ref[...].astype(BF16)

    @pl.when(c == 0)
    def _():
        s_ref[...] = jnp.zeros_like(s_ref)

    row = lax.broadcasted_iota(jnp.int32, (CHUNK, CHUNK), 0)
    col = lax.broadcasted_iota(jnp.int32, (CHUNK, CHUNK), 1)
    causal = row >= col
    for ci in range(z_ref.shape[0] // CHUNK):
        rows = slice(ci * CHUNK, (ci + 1) * CHUNK)
        cos_q, sin_q, cos_k, sin_k = (rot_ref[rows, i * HEAD_DIM:(i + 1) * HEAD_DIM] for i in range(4))
        for h in range(HEADS):
            lo = h * HEAD_DIM
            q = _rotary(z_ref[rows, lo:lo + HEAD_DIM].astype(F32), cos_q, sin_q)
            k = _rotary(z_ref[rows, RET_WIDTH + lo:RET_WIDTH + lo + HEAD_DIM].astype(F32), cos_k, sin_k)
            v = z_ref[rows, 2 * RET_WIDTH + lo:2 * RET_WIDTH + lo + HEAD_DIM]
            g = z_ref[rows, 3 * RET_WIDTH + lo:3 * RET_WIDTH + lo + HEAD_DIM].astype(F32)
            qb = q.astype(BF16)
            kb = k.astype(BF16)
            scores = lax.dot_general(qb, kb, (((1,), (1,)), ((), ())),
                                     preferred_element_type=F32) * dmask_ref[h]
            state = s_ref[h]
            o = (jnp.dot(scores.astype(BF16), v, preferred_element_type=F32)
                 + jnp.dot(qb, state.astype(BF16), preferred_element_type=F32) * cross_ref[h])
            kd = (k * sdec_ref[h]).astype(BF16)
            s_ref[h] = state * cdec[h] + lax.dot_general(kd, v, (((0,), (0,)), ((), ())),
                                                         preferred_element_type=F32)
            o_ref[rows, lo:lo + HEAD_DIM] = _silu_gate(g, _layernorm(o)).astype(o_ref.dtype)

        u = _gelu(z_ref[rows, 4 * RET_WIDTH:4 * RET_WIDTH + SGU_WIDTH].astype(F32))
        vn = (_layernorm(_gelu(z_ref[rows, 4 * RET_WIDTH + SGU_WIDTH:].astype(F32))) * lng_ref[...]
              + lnb_ref[...])
        for gi in range(GROUPS):
            lo = gi * CHUNK
            w = jnp.where(causal, ws_ref[gi], 0.0).astype(BF16)
            s = (jnp.dot(w, vn[:, lo:lo + CHUNK].astype(BF16), preferred_element_type=F32)
                 + bias_ref[:, lo:lo + CHUNK])
            o_ref[rows, RET_WIDTH + lo:RET_WIDTH + lo + CHUNK] = (u[:, lo:lo + CHUNK] * s).astype(o_ref.dtype)

    @pl.when(c == pl.num_programs(1) - 1)
    def _():
        sout_ref[...] = s_ref[...]


MIX_CHUNKS_PER_STEP = 4


def _mix_prompt(z, w_s, bias_tc, ln_g, ln_b, side32, batch, seq):
    rows = MIX_CHUNKS_PER_STEP * CHUNK
    steps_per_batch = seq // rows
    side_spec = pl.BlockSpec((side32.shape[0] // (batch * steps_per_batch), side32.shape[1]),
                             lambda b, c: (b * steps_per_batch + c, 0))
    rot = np.concatenate(_rotary_tables(np.arange(seq)) + _rotary_tables(np.arange(seq), HEAD_DIM ** -0.5),
                         axis=1)
    dmask, crossb, sdecb, cdec = _decay_tables(CHUNK)
    const3 = lambda b, c: (0, 0, 0)
    const2 = lambda b, c: (0, 0)
    kern = functools.partial(_mix_prompt_kernel, cdec=cdec)
    return pl.pallas_call(
        kern,
        grid=(batch, seq // rows),
        in_specs=[
            pl.BlockSpec((None, rows, IN_WIDTH), lambda b, c: (b, c, 0)),
            pl.BlockSpec((rows, 4 * HEAD_DIM), lambda b, c: (c, 0)),
            pl.BlockSpec((HEADS, CHUNK, CHUNK), const3),
            pl.BlockSpec((HEADS, CHUNK, HEAD_DIM), const3),
            pl.BlockSpec((HEADS, CHUNK, HEAD_DIM), const3),
            pl.BlockSpec((GROUPS, CHUNK, CHUNK), const3),
            pl.BlockSpec((CHUNK, SGU_WIDTH), const2),
            pl.BlockSpec((1, SGU_WIDTH), const2),
            pl.BlockSpec((1, SGU_WIDTH), const2),
            side_spec,
        ],
        out_specs=[
            pl.BlockSpec((None, rows, D_MODEL), lambda b, c: (b, c, 0)),
            pl.BlockSpec((None, HEADS, HEAD_DIM, HEAD_DIM), lambda b, c: (b, 0, 0, 0)),
            side_spec,
        ],
        out_shape=[
            jax.ShapeDtypeStruct((batch, seq, D_MODEL), BF16),
            jax.ShapeDtypeStruct((batch, HEADS, HEAD_DIM, HEAD_DIM), F32),
            jax.ShapeDtypeStruct(side32.shape, BF16),
        ],
        scratch_shapes=[pltpu.VMEM((HEADS, HEAD_DIM, HEAD_DIM), F32)],
        compiler_params=pltpu.CompilerParams(
            dimension_semantics=("arbitrary", "arbitrary"), vmem_limit_bytes=VMEM_LIMIT_SMALL),
        name="mix_prompt",
    )(z.reshape(batch, seq, IN_WIDTH), jnp.asarray(rot), jnp.asarray(dmask),
      jnp.asarray(crossb), jnp.asarray(sdecb), w_s, bias_tc, ln_g, ln_b, side32)


def _block_diag(x, mask):
    return jnp.where(mask, jnp.concatenate([x] * HEADS, axis=0), 0.0)


def _heads_to_rows(x):
    return jnp.concatenate([x[:, h * HEAD_DIM:(h + 1) * HEAD_DIM] for h in range(HEADS)], axis=0)


def _mix_sample_kernel(z_ref, state_ref, cos_ref, sin_ref, dmask_ref, cross_ref, sdec_ref, wt_ref,
                       bias_ref, lng_ref, lnb_ref, o_ref, sout_ref, vn_ref, *, cdec, seq):
    bb = z_ref.shape[0]

    u = _gelu(z_ref[:, :, 4 * RET_WIDTH:4 * RET_WIDTH + SGU_WIDTH])
    vn = _layernorm(_gelu(z_ref[:, :, 4 * RET_WIDTH + SGU_WIDTH:])) * lng_ref[...] + lnb_ref[...]
    vn_ref[...] = vn
    trow = lax.broadcasted_iota(jnp.int32, (seq, SGU_WIDTH), 0)
    s = jnp.broadcast_to(bias_ref[...], (bb, seq, SGU_WIDTH))
    for src in range(seq):
        w = jnp.where(trow >= src, wt_ref[src], 0.0)
        s = s + w * vn[:, src:src + 1, :]
    o_ref[:, :, RET_WIDTH:] = u * s

    cos2 = cos_ref[...]
    sin2 = sin_ref[...]
    rblk = lax.broadcasted_iota(jnp.int32, (HEADS * seq, RET_WIDTH), 0) // seq
    cblk = lax.broadcasted_iota(jnp.int32, (HEADS * seq, RET_WIDTH), 1) // HEAD_DIM
    diag = rblk == cblk

    def rotary_heads(base, b):
        return jnp.concatenate(
            [_rotary(z_ref[b, :, base + h * HEAD_DIM:base + (h + 1) * HEAD_DIM], cos2, sin2)
             for h in range(HEADS)], axis=1)

    def body(b, carry):
        q = rotary_heads(0, b)
        k = rotary_heads(RET_WIDTH, b) * (HEAD_DIM ** -0.5)
        v_rows = _heads_to_rows(z_ref[b, :, 2 * RET_WIDTH:3 * RET_WIDTH]).astype(BF16)
        g_rows = _heads_to_rows(z_ref[b, :, 3 * RET_WIDTH:4 * RET_WIDTH])
        qbd = _block_diag(q, diag).astype(BF16)
        kbd = _block_diag(k, diag).astype(BF16)
        kdbd = _block_diag(k * sdec_ref[...], diag).astype(BF16)
        scores = lax.dot_general(qbd, kbd, (((1,), (1,)), ((), ())),
                                 preferred_element_type=F32) * dmask_ref[...]
        state = state_ref[b].reshape(HEADS * HEAD_DIM, HEAD_DIM)
        o = (jnp.dot(scores.astype(BF16), v_rows, preferred_element_type=F32)
             + jnp.dot(qbd, state.astype(BF16), preferred_element_type=F32) * cross_ref[...])
        out = _silu_gate(g_rows, _layernorm(o))
        ds = lax.dot_general(kdbd, v_rows, (((0,), (0,)), ((), ())), preferred_element_type=F32)
        for h in range(HEADS):
            o_ref[b, :, h * HEAD_DIM:(h + 1) * HEAD_DIM] = out[h * seq:(h + 1) * seq]
            sout_ref[b, h] = (state_ref[b, h] * cdec[h]
                              + ds[h * HEAD_DIM:(h + 1) * HEAD_DIM])
        return carry

    lax.fori_loop(0, bb, body, 0, unroll=True)


def _mix_sample(z, state, wt, bias_t, ln_g, ln_b, batch, seq):
    bb = 16
    cos2, sin2 = _rotary_tables(PAST_LEN + np.arange(seq))
    dmask, crossb, sdecb, cdec = _decay_tables(seq)
    dm = np.zeros((HEADS * seq, HEADS * seq), np.float32)
    for h in range(HEADS):
        dm[h * seq:(h + 1) * seq, h * seq:(h + 1) * seq] = dmask[h]
    cross_rows = crossb.reshape(HEADS * seq, HEAD_DIM)
    sdec_cols = np.ascontiguousarray(np.transpose(sdecb, (1, 0, 2)).reshape(seq, RET_WIDTH))
    const3 = lambda i: (0, 0, 0)
    const2 = lambda i: (0, 0)
    kern = functools.partial(_mix_sample_kernel, cdec=cdec, seq=seq)
    return pl.pallas_call(
        kern,
        grid=(batch // bb,),
        in_specs=[
            pl.BlockSpec((bb, seq, IN_WIDTH), lambda i: (i, 0, 0)),
            pl.BlockSpec((bb, HEADS, HEAD_DIM, HEAD_DIM), lambda i: (i, 0, 0, 0)),
            pl.BlockSpec((seq, HEAD_DIM), const2),
            pl.BlockSpec((seq, HEAD_DIM), const2),
            pl.BlockSpec((HEADS * seq, HEADS * seq), const2),
            pl.BlockSpec((HEADS * seq, HEAD_DIM), const2),
            pl.BlockSpec((seq, RET_WIDTH), const2),
            pl.BlockSpec((seq, seq, SGU_WIDTH), const3),
            pl.BlockSpec((seq, SGU_WIDTH), const2),
            pl.BlockSpec((1, SGU_WIDTH), const2),
            pl.BlockSpec((1, SGU_WIDTH), const2),
        ],
        out_specs=[
            pl.BlockSpec((bb, seq, D_MODEL), lambda i: (i, 0, 0)),
            pl.BlockSpec((bb, HEADS, HEAD_DIM, HEAD_DIM), lambda i: (i, 0, 0, 0)),
            pl.BlockSpec((bb, seq, SGU_WIDTH), lambda i: (i, 0, 0)),
        ],
        out_shape=[
            jax.ShapeDtypeStruct((batch, seq, D_MODEL), F32),
            jax.ShapeDtypeStruct((batch, HEADS, HEAD_DIM, HEAD_DIM), F32),
            jax.ShapeDtypeStruct((batch, seq, SGU_WIDTH), F32),
        ],
        compiler_params=pltpu.CompilerParams(
            dimension_semantics=("arbitrary",), vmem_limit_bytes=VMEM_LIMIT_BIG),
        name="mix_sample",
    )(z.reshape(batch, seq, IN_WIDTH), state, jnp.asarray(cos2), jnp.asarray(sin2), jnp.asarray(dm),
      jnp.asarray(cross_rows), jnp.asarray(sdec_cols), wt, bias_t, ln_g, ln_b)


OUT_TAIL_ROWS = 128


def _out_proj_kernel(m_ref, w_ref, x_ref, gpost_ref, gt_ref, gpre_ref, sc_ref, sh_ref,
                     x1_ref, h2_ref, *wbf_out):
    tm = x_ref.shape[0]
    if wbf_out:
        w_ref, w32_ref = wbf_out[0], w_ref

        @pl.when(pl.program_id(0) == 0)
        def _():
            w_ref[...] = w32_ref[...].astype(BF16)
    bounds = [0, tm - OUT_TAIL_ROWS, tm] if tm > OUT_TAIL_ROWS else [0, tm]
    for lo, hi in zip(bounds[:-1], bounds[1:]):
        rows = slice(lo, hi)
        m = jnp.dot(m_ref[rows, :].astype(BF16), w_ref[...], preferred_element_type=F32)
        gate = _mod_rows(gt_ref, rows, tm, lambda a: a * gpost_ref[...])
        x1 = x_ref[rows, :] + m * _rms_scale(m) * gate
        x1_ref[rows, :] = x1
        scale = _mod_rows(sc_ref, rows, tm, lambda a: gpre_ref[...] * (1.0 + a))
        h2 = x1 * _rms_scale(x1) * scale + _mod_rows(sh_ref, rows, tm)
        h2_ref[rows, :] = h2.astype(BF16)


def _out_proj(m, w, x, gpost, gt, gpre, sc, sh, *, tm, tiles_per_mod):
    t = x.shape[0]
    mod_idx = lambda i: (i // tiles_per_mod, 0, 0)
    vec = pl.BlockSpec((1, D_MODEL), lambda i: (0, 0))
    w_spec = pl.BlockSpec((D_MODEL, D_MODEL), lambda i: (0, 0))
    cast_w = w.dtype != BF16
    return pl.pallas_call(
        _out_proj_kernel,
        grid=(t // tm,),
        in_specs=[
            pl.BlockSpec((tm, D_MODEL), lambda i: (i, 0)),
            w_spec,
            pl.BlockSpec((tm, D_MODEL), lambda i: (i, 0)),
            vec,
            _mod_spec(gt, tm, mod_idx),
            vec,
            _mod_spec(sc, tm, mod_idx),
            _mod_spec(sh, tm, mod_idx),
        ],
        out_specs=[
            pl.BlockSpec((tm, D_MODEL), lambda i: (i, 0)),
            pl.BlockSpec((tm, D_MODEL), lambda i: (i, 0)),
        ] + [w_spec] * cast_w,
        out_shape=[
            jax.ShapeDtypeStruct((t, D_MODEL), F32),
            jax.ShapeDtypeStruct((t, D_MODEL), BF16),
        ] + [jax.ShapeDtypeStruct(w.shape, BF16)] * cast_w,
        compiler_params=pltpu.CompilerParams(
            dimension_semantics=("arbitrary",), vmem_limit_bytes=VMEM_LIMIT_BIG),
        name="out_proj_cast" if cast_w else "out_proj",
    )(m, w, x, gpost, gt[0], gpre, sc[0], sh[0])


FFN_TC = 1024
FFN_TC_CAST = 512
FFN_ROWS = 512
FFN_EPI_ROWS = 256


def _ffn_kernel(h2_ref, w1_ref, w2_ref, x1_hbm, g_ref, gt_ref, y_ref, *rest, cast1, cast2):
    *wbf_out, x1_buf, x1_sem = rest
    if cast1:
        w1_ref, w1_32 = wbf_out[0], w1_ref
        w1_ref[...] = w1_32[...].astype(BF16)
    if cast2:
        w2_ref, w2_32 = wbf_out[-1], w2_ref
        w2_ref[...] = w2_32[...].astype(BF16)
    i = pl.program_id(0)
    k = pl.program_id(1)
    last = pl.num_programs(1) - 1
    tm = y_ref.shape[0]
    n_epi = tm // FFN_EPI_ROWS

    def x1_copy(r):
        row0 = pl.multiple_of(i * tm + r * FFN_EPI_ROWS, FFN_EPI_ROWS)
        return pltpu.make_async_copy(x1_hbm.at[pl.ds(row0, FFN_EPI_ROWS), :], x1_buf.at[r], x1_sem.at[r])

    def partial_product(rows):
        a = jnp.dot(h2_ref[rows, :], w1_ref[...], preferred_element_type=F32)
        a = jnp.square(jnp.maximum(a, 0.0)).astype(BF16)
        return jnp.dot(a, w2_ref[...], preferred_element_type=F32)

    chunks = [slice(r * FFN_ROWS, (r + 1) * FFN_ROWS) for r in range(tm // FFN_ROWS)]

    @pl.when(k == last - 1)
    def _():
        for r in range(n_epi):
            x1_copy(r).start()

    @pl.when(k == 0)
    def _():
        for rows in chunks:
            y_ref[rows, :] = partial_product(rows)

    @pl.when(jnp.logical_and(k > 0, k < last))
    def _():
        for rows in chunks:
            y_ref[rows, :] += partial_product(rows)

    @pl.when(k == last)
    def _():
        for r in range(n_epi):
            x1_copy(r).wait()
        for c, rows in enumerate(chunks):
            y_ref[rows, :] += partial_product(rows)
            for r in range(c * FFN_ROWS // FFN_EPI_ROWS, (c + 1) * FFN_ROWS // FFN_EPI_ROWS):
                erows = slice(r * FFN_EPI_ROWS, (r + 1) * FFN_EPI_ROWS)
                f = y_ref[erows, :]
                gate = _mod_rows(gt_ref, erows, tm, lambda a: a * g_ref[...])
                y_ref[erows, :] = x1_buf[r] + f * _rms_scale(f) * gate


def _ffn(h2, w1, w2, x1, g, gt, *, tm, tiles_per_mod):
    t = h2.shape[0]
    cast1, cast2 = w1.dtype != BF16, w2.dtype != BF16
    tc = FFN_TC_CAST if cast1 or cast2 else FFN_TC
    assert not (cast1 or cast2) or t == tm
    w1_spec = pl.BlockSpec((D_MODEL, tc), lambda i, k: (0, k))
    w2_spec = pl.BlockSpec((tc, D_MODEL), lambda i, k: (k, 0))
    assert tm % FFN_ROWS == 0 and FFN_ROWS % FFN_EPI_ROWS == 0 and tm // FFN_EPI_ROWS >= 2
    assert D_FF // tc >= 2
    mod_idx = lambda i, k: (i // tiles_per_mod, 0, 0)
    return pl.pallas_call(
        functools.partial(_ffn_kernel, cast1=cast1, cast2=cast2),
        grid=(t // tm, D_FF // tc),
        in_specs=[
            pl.BlockSpec((tm, D_MODEL), lambda i, k: (i, 0)),
            w1_spec,
            w2_spec,
            pl.BlockSpec(memory_space=pl.ANY),
            pl.BlockSpec((1, D_MODEL), lambda i, k: (0, 0)),
            _mod_spec(gt, tm, mod_idx),
        ],
        out_specs=[pl.BlockSpec((tm, D_MODEL), lambda i, k: (i, 0))] + [w1_spec] * cast1 + [w2_spec] * cast2,
        out_shape=[jax.ShapeDtypeStruct((t, D_MODEL), F32)]
                  + [jax.ShapeDtypeStruct(w1.shape, BF16)] * cast1 + [jax.ShapeDtypeStruct(w2.shape, BF16)] * cast2,
        scratch_shapes=[pltpu.VMEM((tm // FFN_EPI_ROWS, FFN_EPI_ROWS, D_MODEL), F32),
                        pltpu.SemaphoreType.DMA((tm // FFN_EPI_ROWS,))],
        compiler_params=pltpu.CompilerParams(
            dimension_semantics=("arbitrary", "arbitrary"), vmem_limit_bytes=VMEM_LIMIT_BIG),
        name="ffn_cast" if cast1 or cast2 else "ffn",
    )(h2, w1, w2, x1, g, gt[0])


TM_IN = 1024
TM_OUT = 512
TM_OUT_SHORT = 256
TM_FFN = 1024


def kernel(x_prompt, x_sample, state_ret, c_prompt, c_sample, w_ada, b_ada, g_pre_mix, g_post_mix,
           g_pre_ffn, g_post_ffn, w_in, w_s, b_s, ln_g, ln_b, w_o, w_ff1, w_ff2):
    depth = w_ada.shape[0]
    batch, seq, _ = x_prompt.shape
    dec_batch, dec_seq, _ = x_sample.shape
    yp = x_prompt.reshape(batch * seq, D_MODEL)
    ys = x_sample.reshape(dec_batch * dec_seq, D_MODEL)
    sp_list, ss_list, vs_list = [], [], []
    for l in range(depth):
        mod_s, mod_p = _ada(c_sample, c_prompt, w_ada[l], b_ada[l])
        mod_p = mod_p.reshape(batch, 1, 6 * D_MODEL)
        n_s = dec_batch * dec_seq
        prompt_mod = lambda col: (mod_p, col)
        sample_mod = lambda col, tm: (mod_s.reshape(n_s // tm, tm // dec_seq, 6 * D_MODEL), col)
        SH1, SC1, GT1, SH2, SC2, GT2 = range(6)
        bias_tc = jnp.repeat(b_s[l].T, CHUNK, axis=1)
        wt = jnp.repeat(jnp.transpose(w_s[l][:, :dec_seq, :dec_seq], (2, 1, 0)), CHUNK, axis=2)
        bias_t = bias_tc[:dec_seq]
        row = lambda v: v.reshape(1, -1)
        g_pre_mix_l, g_post_mix_l, g_pre_ffn_l, g_post_ffn_l = (
            row(g[l]) for g in (g_pre_mix, g_post_mix, g_pre_ffn, g_post_ffn))
        ln_g_l, ln_b_l = row(ln_g[l]), row(ln_b[l])
        zs, w_in_bf = _norm_matmul_cast(ys, g_pre_mix_l, sample_mod(SC1, n_s), sample_mod(SH1, n_s), w_in[l],
                                        tiles_per_mod=1)
        mix_s, ss, vn = _mix_sample(zs, state_ret[l], wt, bias_t, ln_g_l, ln_b_l, dec_batch, dec_seq)
        x1_s, h2_s, w_o_bf = _out_proj(mix_s.reshape(n_s, D_MODEL), w_o[l], ys, g_post_mix_l,
                                       sample_mod(GT1, TM_OUT_SHORT), g_pre_ffn_l,
                                       sample_mod(SC2, TM_OUT_SHORT), sample_mod(SH2, TM_OUT_SHORT),
                                       tm=TM_OUT_SHORT, tiles_per_mod=1)
        zp = _norm_matmul(yp, g_pre_mix_l, prompt_mod(SC1), prompt_mod(SH1), w_in_bf, tm=TM_IN,
                          tiles_per_mod=seq // TM_IN)
        mix_p, sp, w_ff2_bf = _mix_prompt(zp, w_s[l], bias_tc, ln_g_l, ln_b_l, w_ff2[l], batch, seq)
        ys, w_ff1_bf = _ffn(h2_s, w_ff1[l], w_ff2_bf, x1_s, g_post_ffn_l, sample_mod(GT2, TM_FFN),
                            tm=TM_FFN, tiles_per_mod=1)
        x1_p, h2_p = _out_proj(mix_p.reshape(batch * seq, D_MODEL), w_o_bf, yp, g_post_mix_l,
                               prompt_mod(GT1), g_pre_ffn_l, prompt_mod(SC2), prompt_mod(SH2), tm=TM_OUT,
                               tiles_per_mod=seq // TM_OUT)
        (yp,) = _ffn(h2_p, w_ff1_bf, w_ff2_bf, x1_p, g_post_ffn_l, prompt_mod(GT2), tm=TM_FFN,
                     tiles_per_mod=seq // TM_FFN)
        sp_list.append(sp)
        ss_list.append(ss)
        vs_list.append(vn)
    return (yp.reshape(batch, seq, D_MODEL), ys.reshape(dec_batch, dec_seq, D_MODEL),
            jnp.stack(sp_list), jnp.stack(ss_list), jnp.stack(vs_list))
```
